```python
import jax
import jax.numpy as jnp
from jax import lax
import numpy as np


D_MODEL = 4096
BATCH = 4
SEQ = 2048
DEPTH = 1

GRID_W = 64
NA_HEADS = 16
NA_HEAD_DIM = 128
NA_WIN_ROWS = 8
NA_WIN_COLS = 16
MLA_HEADS = 16
MLA_Q_RANK = 1024
MLA_KV_RANK = 512
MLA_NOPE_DIM = 128
MLA_ROPE_DIM = 64
MLA_V_DIM = 128
ROPE_THETA = 10000.0
Q_BLOCK = 128
N_EXPERTS = 32
TOP_K = 4
D_EXPERT = 1536
SWIGLU_LIMIT = 7.0
SWIGLU_ALPHA = 1.702
NORM_EPS = 1e-6
NA_WIDTH = NA_HEADS * NA_HEAD_DIM
MLA_QK_DIM = MLA_NOPE_DIM + MLA_ROPE_DIM
MLA_WIDTH = MLA_HEADS * MLA_V_DIM
IN_SPLIT_SIZES = (NA_WIDTH, NA_WIDTH, NA_WIDTH, MLA_Q_RANK, MLA_KV_RANK, MLA_ROPE_DIM, D_MODEL, D_MODEL)
IN_COLS = sum(IN_SPLIT_SIZES)

kernel_name = 'hybrid_natten_mla_moe_encoder_block'


def rms_norm(x, g):
    xf = x.astype(jnp.float32)
    y = xf * lax.rsqrt(jnp.mean(xf * xf, axis=-1, keepdims=True) + NORM_EPS)
    return (y * g.astype(jnp.float32)).astype(x.dtype)


def apply_rope(x, positions):
    half = MLA_ROPE_DIM // 2
    inv_freq = ROPE_THETA ** (-(jnp.arange(half, dtype=jnp.float32) * 2.0) / MLA_ROPE_DIM)
    ang = positions.astype(jnp.float32)[..., None] * inv_freq
    cos = jnp.cos(ang)[:, :, None, :]
    sin = jnp.sin(ang)[:, :, None, :]
    xf = x.astype(jnp.float32)
    x1, x2 = xf[..., :half], xf[..., half:]
    return jnp.concatenate([x1 * cos - x2 * sin, x2 * cos + x1 * sin], axis=-1).astype(x.dtype)


def neighborhood_attention(q, k, v, rpb):
    b, s, h, dh = q.shape
    rows = s // GRID_W
    kh = min(NA_WIN_ROWS, rows)
    scale = dh ** -0.5
    qg = q.reshape(b, rows, GRID_W, h, dh)
    kg = k.reshape(b, rows, GRID_W, h, dh)
    vg = v.reshape(b, rows, GRID_W, h, dh)
    cols = jnp.arange(GRID_W)
    col_start = jnp.clip(cols - NA_WIN_COLS // 2, 0, GRID_W - NA_WIN_COLS)
    col_idx = col_start[:, None] + jnp.arange(NA_WIN_COLS)[None, :]
    rpb_cols = rpb[:, :, col_idx - cols[:, None] + NA_WIN_COLS - 1]

    def one_row(r):
        rs = jnp.clip(r - kh // 2, 0, rows - kh)
        k_win = lax.dynamic_slice_in_dim(kg, rs, kh, axis=1)[:, :, col_idx]
        v_win = lax.dynamic_slice_in_dim(vg, rs, kh, axis=1)[:, :, col_idx]
        q_row = lax.dynamic_index_in_dim(qg, r, axis=1, keepdims=False)
        dr = rs + jnp.arange(kh) - r
        bias = jnp.transpose(rpb_cols[:, dr + NA_WIN_ROWS - 1], (0, 2, 1, 3))
        sc = jnp.einsum('bwhd,biwjhd->bhwij', q_row, k_win).astype(jnp.float32) * scale + bias.astype(jnp.float32)
        p = jax.nn.softmax(sc.reshape(b, h, GRID_W, kh * NA_WIN_COLS), axis=-1)
        p = p.reshape(b, h, GRID_W, kh, NA_WIN_COLS).astype(v.dtype)
        return jnp.einsum('bhwij,biwjhd->bwhd', p, v_win)

    out = lax.map(one_row, jnp.arange(rows))
    return jnp.transpose(out, (1, 0, 2, 3, 4)).reshape(b, s, h * dh)


def latent_attention(q, k, v):
    b, s, h, dqk = q.shape
    nb = s // Q_BLOCK
    scale = dqk ** -0.5
    q_blocks = jnp.transpose(q.reshape(b, nb, Q_BLOCK, h, dqk), (1, 0, 2, 3, 4))

    def one_block(qb):
        sc = jnp.einsum('bqhd,bkhd->bhqk', qb, k).astype(jnp.float32) * scale
        p = jax.nn.softmax(sc, axis=-1).astype(v.dtype)
        return jnp.einsum('bhqk,bkhd->bqhd', p, v)

    out = lax.map(one_block, q_blocks)
    return jnp.transpose(out, (1, 0, 2, 3, 4)).reshape(b, s, h * v.shape[-1])


def moe_ffn(h, layer, w_router, b_router, w_gate_up, b_gate_up, w_down, b_down):
    logits = (h @ w_router[layer] + b_router[layer]).astype(jnp.float32)
    top_vals, top_idx = lax.top_k(logits, TOP_K)
    top_w = jax.nn.softmax(top_vals, axis=-1)
    gates = jnp.sum(jax.nn.one_hot(top_idx, N_EXPERTS, dtype=jnp.float32) * top_w[..., None], axis=-2).astype(h.dtype)
    out = jnp.zeros_like(h)
    for e in range(N_EXPERTS):
        gu = h @ w_gate_up[layer, e] + b_gate_up[layer, e]
        gate = jnp.minimum(gu[..., 0::2], SWIGLU_LIMIT)
        up = jnp.clip(gu[..., 1::2], -SWIGLU_LIMIT, SWIGLU_LIMIT)
        act = (up + 1.0) * gate * jax.nn.sigmoid(SWIGLU_ALPHA * gate)
        out = out + gates[..., e:e + 1] * (act @ w_down[layer, e] + b_down[layer, e])
    return out


def setup_inputs(seed: int = 0) -> dict:
    key = jax.random.key(seed)
    ks = jax.random.split(key, 20)

    def normal(k, shape, scale):
        return jax.random.normal(k, shape, jnp.float32) * scale

    L = DEPTH
    x = normal(ks[0], (BATCH, SEQ, D_MODEL), 1.0)
    positions = jnp.tile(jnp.arange(SEQ, dtype=jnp.int32)[None, :], (BATCH, 1))
    norm_mix = 1.0 + normal(ks[1], (L, D_MODEL), 0.02)
    w_in = normal(ks[2], (L, D_MODEL, IN_COLS), D_MODEL ** -0.5)
    q_a_norm = 1.0 + normal(ks[3], (L, MLA_Q_RANK), 0.02)
    w_q_b = normal(ks[4], (L, MLA_Q_RANK, MLA_HEADS * MLA_QK_DIM), MLA_Q_RANK ** -0.5)
    kv_a_norm = 1.0 + normal(ks[5], (L, MLA_KV_RANK), 0.02)
    w_kv_b = normal(ks[6], (L, MLA_KV_RANK, MLA_HEADS * (MLA_NOPE_DIM + MLA_V_DIM)), MLA_KV_RANK ** -0.5)
    na_rpb = normal(ks[7], (L, NA_HEADS, 2 * NA_WIN_ROWS - 1, 2 * NA_WIN_COLS - 1), 0.5)
    w_proj_a = normal(ks[8], (L, NA_WIDTH, D_MODEL), NA_WIDTH ** -0.5)
    w_proj_b = normal(ks[9], (L, MLA_WIDTH, D_MODEL), MLA_WIDTH ** -0.5)
    w_out = normal(ks[10], (L, D_MODEL, D_MODEL), D_MODEL ** -0.5)
    norm_ffn = 1.0 + normal(ks[11], (L, D_MODEL), 0.02)
    w_router = normal(ks[12], (L, D_MODEL, N_EXPERTS), D_MODEL ** -0.5)
    b_router = normal(ks[13], (L, N_EXPERTS), 0.01)
    w_gate_up = normal(ks[14], (L, N_EXPERTS, D_MODEL, 2 * D_EXPERT), D_MODEL ** -0.5)
    b_gate_up = normal(ks[15], (L, N_EXPERTS, 2 * D_EXPERT), 0.02)
    w_down = normal(ks[16], (L, N_EXPERTS, D_EXPERT, D_MODEL), D_EXPERT ** -0.5)
    b_down = normal(ks[17], (L, N_EXPERTS, D_MODEL), 0.02)
    norm_final = 1.0 + normal(ks[18], (D_MODEL,), 0.02)
    return {'x': x, 'positions': positions, 'norm_mix': norm_mix, 'w_in': w_in,
            'q_a_norm': q_a_norm, 'w_q_b': w_q_b, 'kv_a_norm': kv_a_norm, 'w_kv_b': w_kv_b,
            'na_rpb': na_rpb, 'w_proj_a': w_proj_a, 'w_proj_b': w_proj_b, 'w_out': w_out,
            'norm_ffn': norm_ffn, 'w_router': w_router, 'b_router': b_router,
            'w_gate_up': w_gate_up, 'b_gate_up': b_gate_up, 'w_down': w_down, 'b_down': b_down,
            'norm_final': norm_final}


def reference(x, positions, norm_mix, w_in, q_a_norm, w_q_b, kv_a_norm, w_kv_b, na_rpb,
              w_proj_a, w_proj_b, w_out, norm_ffn, w_router, b_router, w_gate_up, b_gate_up,
              w_down, b_down, norm_final):
    b, s = x.shape[0], x.shape[1]
    split_points = [int(p) for p in np.cumsum(IN_SPLIT_SIZES)[:-1]]
    for l in range(DEPTH):
        h = rms_norm(x, norm_mix[l])
        proj = h @ w_in[l]
        qa, ka, va, cq, ckv, kr, ga, gb = jnp.split(proj, split_points, axis=-1)

        oa = neighborhood_attention(qa.reshape(b, s, NA_HEADS, NA_HEAD_DIM),
                                    ka.reshape(b, s, NA_HEADS, NA_HEAD_DIM),
                                    va.reshape(b, s, NA_HEADS, NA_HEAD_DIM), na_rpb[l])

        q = (rms_norm(cq, q_a_norm[l]) @ w_q_b[l]).reshape(b, s, MLA_HEADS, MLA_QK_DIM)
        q_nope, q_pe = q[..., :MLA_NOPE_DIM], apply_rope(q[..., MLA_NOPE_DIM:], positions)
        k_pe = apply_rope(kr[:, :, None, :], positions)
        kv = (rms_norm(ckv, kv_a_norm[l]) @ w_kv_b[l]).reshape(b, s, MLA_HEADS, MLA_NOPE_DIM + MLA_V_DIM)
        k_nope, v = kv[..., :MLA_NOPE_DIM], kv[..., MLA_NOPE_DIM:]
        q_full = jnp.concatenate([q_nope, q_pe], axis=-1)
        k_full = jnp.concatenate([k_nope, jnp.broadcast_to(k_pe, (b, s, MLA_HEADS, MLA_ROPE_DIM))], axis=-1)
        ob = latent_attention(q_full, k_full, v)

        y = jax.nn.sigmoid(ga) * (oa @ w_proj_a[l]) + jax.nn.sigmoid(gb) * (ob @ w_proj_b[l])
        x = x + y @ w_out[l]

        x = x + moe_ffn(rms_norm(x, norm_ffn[l]), l, w_router, b_router, w_gate_up, b_gate_up, w_down, b_down)
    return rms_norm(x, norm_final)
```

```python
import functools

import numpy as np
import jax
import jax.numpy as jnp
from jax import lax
from jax.experimental import pallas as pl
from jax.experimental.pallas import tpu as pltpu

F32 = jnp.float32
BF16 = jnp.bfloat16
U32 = jnp.uint32
I32 = jnp.int32

GRID_W = 64
NA_HEADS = 16
NA_HEAD_DIM = 128
NA_WIN_ROWS = 8
NA_WIN_COLS = 16
MLA_HEADS = 16
MLA_NOPE_DIM = 128
MLA_ROPE_DIM = 64
MLA_V_DIM = 128
ROPE_THETA = 10000.0
TOP_K = 4
SWIGLU_LIMIT = 7.0
SWIGLU_ALPHA = 1.702
NORM_EPS = 1e-6

NEG_BIG = -1e30

V7X_VMEM_BYTES = 64 * 1024 * 1024
VMEM_LIMIT = V7X_VMEM_BYTES - 4 * 1024 * 1024

SUB = 256
ITEM_SUBS = 6
ITEM_ROWS = SUB * ITEM_SUBS

_NN = (((1,), (0,)), ((), ()))
_NT = (((1,), (1,)), ((), ()))


def _params(sem=None):
    return pltpu.CompilerParams(vmem_limit_bytes=VMEM_LIMIT, dimension_semantics=sem)


def _rms(x, g):
    return x * lax.rsqrt(jnp.mean(x * x, axis=-1, keepdims=True) + NORM_EPS) * g


def _rmsnorm_body(x_ref, g_ref, o_ref):
    o_ref[...] = _rms(x_ref[...], g_ref[...]).astype(o_ref.dtype)


def _rmsnorm(x, g, *, tm=512):
    t, d = x.shape
    return pl.pallas_call(
        _rmsnorm_body,
        grid=(t // tm,),
        in_specs=[pl.BlockSpec((tm, d), lambda i: (i, 0)),
                  pl.BlockSpec((1, d), lambda i: (0, 0))],
        out_specs=pl.BlockSpec((tm, d), lambda i: (i, 0)),
        out_shape=jax.ShapeDtypeStruct((t, d), BF16),
        compiler_params=_params(("arbitrary",)),
        name="rmsnorm",
    )(x, g.reshape(1, d))


def _mm_body(a_ref, w_ref, o_ref, *, act):
    acc = lax.dot_general(a_ref[...], w_ref[...].astype(BF16), _NN, preferred_element_type=F32)
    if act == "sigmoid":
        acc = jax.nn.sigmoid(acc)
    o_ref[...] = acc.astype(o_ref.dtype)


def _matmul(a, w, *, col0, ncols, tm, tn, out_dtype, act=None, name):
    t, k = a.shape
    return pl.pallas_call(
        functools.partial(_mm_body, act=act),
        grid=(t // tm, ncols // tn),
        in_specs=[pl.BlockSpec((tm, k), lambda i, j: (i, 0), pipeline_mode=pl.Buffered(1)),
                  pl.BlockSpec((k, tn), lambda i, j: (0, j + col0 // tn))],
        out_specs=pl.BlockSpec((tm, tn), lambda i, j: (i, j)),
        out_shape=jax.ShapeDtypeStruct((t, ncols), out_dtype),
        compiler_params=_params(("arbitrary", "arbitrary")),
        name=name,
    )(a, w)


def _qproj_body(a_ref, w_ref, cos_ref, sin_ref, o_ref, *, first_rope_block, tn):
    acc = lax.dot_general(a_ref[...], w_ref[...].astype(BF16), _NN, preferred_element_type=F32)
    j = pl.program_id(1)

    @pl.when(j < first_rope_block)
    def _():
        o_ref[...] = acc.astype(o_ref.dtype)

    @pl.when(j >= first_rope_block)
    def _():
        reps = tn // 128
        c = jnp.concatenate([cos_ref[...]] * reps, axis=1)
        s = jnp.concatenate([sin_ref[...]] * reps, axis=1)
        lane = lax.broadcasted_iota(I32, acc.shape, 1)
        first = (lane & (MLA_ROPE_DIM - 1)) < MLA_ROPE_DIM // 2
        half = MLA_ROPE_DIM // 2
        partner = jnp.where(first, pltpu.roll(acc, tn - half, 1), pltpu.roll(acc, half, 1))
        o_ref[...] = (acc * c + partner * s).astype(o_ref.dtype)


def _qproj(a, w, cos, sin, *, rope_col0, tm, tn):
    t, k = a.shape
    n = w.shape[1]
    return pl.pallas_call(
        functools.partial(_qproj_body, first_rope_block=rope_col0 // tn, tn=tn),
        grid=(t // tm, n // tn),
        in_specs=[pl.BlockSpec((tm, k), lambda i, j: (i, 0), pipeline_mode=pl.Buffered(1)),
                  pl.BlockSpec((k, tn), lambda i, j: (0, j)),
                  pl.BlockSpec((tm, 128), lambda i, j: (i, 0)),
                  pl.BlockSpec((tm, 128), lambda i, j: (i, 0))],
        out_specs=pl.BlockSpec((tm, tn), lambda i, j: (i, j)),
        out_shape=jax.ShapeDtypeStruct((t, n), BF16),
        compiler_params=_params(("arbitrary", "arbitrary")),
        name="q_proj_rope",
    )(a, w, cos, sin)


def _merge_body(oa_ref, ob_ref, wa_ref, wb_ref, ga_ref, gb_ref, o_ref):
    pa = lax.dot_general(oa_ref[...], wa_ref[...].astype(BF16), _NN, preferred_element_type=F32)
    pb = lax.dot_general(ob_ref[...], wb_ref[...].astype(BF16), _NN, preferred_element_type=F32)
    o_ref[...] = (ga_ref[...] * pa + gb_ref[...] * pb).astype(o_ref.dtype)


def _merge(oa, ob, wa, wb, gates, *, tm, tn):
    t, k = oa.shape
    n = wa.shape[1]
    nb = n // tn
    return pl.pallas_call(
        _merge_body,
        grid=(t // tm, nb),
        in_specs=[pl.BlockSpec((tm, k), lambda i, j: (i, 0), pipeline_mode=pl.Buffered(1)),
                  pl.BlockSpec((tm, k), lambda i, j: (i, 0), pipeline_mode=pl.Buffered(1)),
                  pl.BlockSpec((k, tn), lambda i, j: (0, j)),
                  pl.BlockSpec((k, tn), lambda i, j: (0, j)),
                  pl.BlockSpec((tm, tn), lambda i, j: (i, j)),
                  pl.BlockSpec((tm, tn), lambda i, j: (i, j + nb))],
        out_specs=pl.BlockSpec((tm, tn), lambda i, j: (i, j)),
        out_shape=jax.ShapeDtypeStruct((t, n), BF16),
        compiler_params=_params(("arbitrary", "arbitrary")),
        name="gated_merge",
    )(oa, ob, wa, wb, gates, gates)


def _outproj_body(a_ref, w_ref, r_ref, o_ref):
    acc = lax.dot_general(a_ref[...], w_ref[...].astype(BF16), _NN, preferred_element_type=F32)
    o_ref[...] = r_ref[...] + acc


def _outproj(a, w, resid, *, tm, tn):
    t, k = a.shape
    n = w.shape[1]
    return pl.pallas_call(
        _outproj_body,
        grid=(t // tm, n // tn),
        in_specs=[pl.BlockSpec((tm, k), lambda i, j: (i, 0), pipeline_mode=pl.Buffered(1)),
                  pl.BlockSpec((k, tn), lambda i, j: (0, j)),
                  pl.BlockSpec((tm, tn), lambda i, j: (i, j))],
        out_specs=pl.BlockSpec((tm, tn), lambda i, j: (i, j)),
        out_shape=jax.ShapeDtypeStruct((t, n), F32),
        compiler_params=_params(("arbitrary", "arbitrary")),
        name="out_proj_residual",
    )(a, w, resid)


def _mla_prep_body(lat_ref, pos_ref, qn_ref, kvn_ref, invf_ref,
                   cq_ref, ckv_ref, kpe_ref, cos_ref, sin_ref, *, q_rank, kv_rank):
    cq_ref[...] = _rms(lat_ref[:, 0:q_rank], qn_ref[...]).astype(cq_ref.dtype)
    ckv_ref[...] = _rms(lat_ref[:, q_rank:q_rank + kv_rank], kvn_ref[...]).astype(ckv_ref.dtype)
    kr = lat_ref[:, q_rank + kv_rank:q_rank + kv_rank + 128]
    ang = pos_ref[...].astype(F32) * invf_ref[...]
    c = jnp.cos(ang)
    s = jnp.sin(ang)
    lane = lax.broadcasted_iota(I32, ang.shape, 1)
    half = MLA_ROPE_DIM // 2
    first = (lane & (MLA_ROPE_DIM - 1)) < half
    s = jnp.where(first, -s, s)
    partner = jnp.where(first, pltpu.roll(kr, 128 - half, 1), pltpu.roll(kr, half, 1))
    kpe = jnp.where(lane < MLA_ROPE_DIM, kr * c + partner * s, 0.0)
    kpe_ref[:, 0:128] = kpe.astype(kpe_ref.dtype)
    kpe_ref[:, 128:256] = pltpu.roll(kpe, MLA_ROPE_DIM, 1).astype(kpe_ref.dtype)
    cos_ref[...] = c
    sin_ref[...] = s


def _mla_prep(lat, positions, q_norm, kv_norm, inv_freq128, *, tm=512):
    t = lat.shape[0]
    q_rank, kv_rank = q_norm.shape[0], kv_norm.shape[0]
    return pl.pallas_call(
        functools.partial(_mla_prep_body, q_rank=q_rank, kv_rank=kv_rank),
        grid=(t // tm,),
        in_specs=[pl.BlockSpec((tm, lat.shape[1]), lambda i: (i, 0)),
                  pl.BlockSpec((tm, 1), lambda i: (i, 0)),
                  pl.BlockSpec((1, q_rank), lambda i: (0, 0)),
                  pl.BlockSpec((1, kv_rank), lambda i: (0, 0)),
                  pl.BlockSpec((1, 128), lambda i: (0, 0))],
        out_specs=[pl.BlockSpec((tm, q_rank), lambda i: (i, 0)),
                   pl.BlockSpec((tm, kv_rank), lambda i: (i, 0)),
                   pl.BlockSpec((tm, 256), lambda i: (i, 0)),
                   pl.BlockSpec((tm, 128), lambda i: (i, 0)),
                   pl.BlockSpec((tm, 128), lambda i: (i, 0))],
        out_shape=[jax.ShapeDtypeStruct((t, q_rank), BF16),
                   jax.ShapeDtypeStruct((t, kv_rank), BF16),
                   jax.ShapeDtypeStruct((t, 256), BF16),
                   jax.ShapeDtypeStruct((t, 128), F32),
                   jax.ShapeDtypeStruct((t, 128), F32)],
        compiler_params=_params(("arbitrary",)),
        name="mla_prep",
    )(lat, positions.reshape(t, 1), q_norm.reshape(1, -1), kv_norm.reshape(1, -1), inv_freq128)


def _mla_body(qn_ref, qp_ref, kv_ref, kpe_ref, o_ref, kf_ref, vt_ref, *, scale):
    qi = pl.program_id(2)

    @pl.when(qi == 0)
    def _():
        for hh in range(2):
            kf_ref[hh, :, 0:128] = kv_ref[:, 256 * hh:256 * hh + 128]
            kf_ref[hh, :, 128:256] = kpe_ref[:, 128 * hh:128 * hh + 128]
            v = kv_ref[:, 256 * hh + 128:256 * hh + 256]
            vt_ref[hh] = v.astype(F32).T.astype(BF16)

    lane = lax.broadcasted_iota(I32, qp_ref.shape, 1)
    for hh in range(2):
        qp = qp_ref[...]
        keep = (lane >= MLA_ROPE_DIM) if hh else (lane < MLA_ROPE_DIM)
        qp = jnp.where(keep, qp, jnp.zeros_like(qp))
        qf = jnp.concatenate([qn_ref[:, 128 * hh:128 * hh + 128], qp], axis=1)
        st = lax.dot_general(kf_ref[hh], qf, _NT, preferred_element_type=F32) * scale
        m = jnp.max(st, axis=0, keepdims=True)
        e = jnp.exp(st - m)
        l = jnp.sum(e, axis=0, keepdims=True)
        ot = lax.dot_general(vt_ref[hh], e.astype(BF16), _NN, preferred_element_type=F32)
        o_ref[:, 128 * hh:128 * hh + 128] = (ot / l).T.astype(o_ref.dtype)


def _mla_attention(q2, kv, kpe, *, batch, seq, tq=256):
    t = q2.shape[0]
    nq = seq // tq
    hp = MLA_HEADS // 2
    nope_w = MLA_HEADS * MLA_NOPE_DIM
    scale = float((MLA_NOPE_DIM + MLA_ROPE_DIM) ** -0.5)
    return pl.pallas_call(
        functools.partial(_mla_body, scale=scale),
        grid=(batch, hp, nq),
        in_specs=[pl.BlockSpec((tq, 256), lambda b, h, q: (b * nq + q, h)),
                  pl.BlockSpec((tq, 128), lambda b, h, q: (b * nq + q, nope_w // 128 + h)),
                  pl.BlockSpec((seq, 512), lambda b, h, q: (b, h)),
                  pl.BlockSpec((seq, 256), lambda b, h, q: (b, 0))],
        out_specs=pl.BlockSpec((tq, 256), lambda b, h, q: (b * nq + q, h)),
        out_shape=jax.ShapeDtypeStruct((t, MLA_HEADS * MLA_V_DIM), BF16),
        scratch_shapes=[pltpu.VMEM((2, seq, 256), BF16), pltpu.VMEM((2, 128, seq), BF16)],
        compiler_params=_params(("arbitrary", "arbitrary", "arbitrary")),
        name="mla_attention",
    )(q2, q2, kv, kpe)


NA_Q_ROWS = 4
NA_K_ROWS = 12


def _na_group(g, rows):
    r0 = NA_Q_ROWS * g
    w0 = min(max(r0 - NA_WIN_ROWS // 2, 0), rows - NA_K_ROWS)
    if g == 0:
        var = 0
    elif g == rows // NA_Q_ROWS - 1:
        var = 2
    else:
        var = 1
    return r0, w0, var


def _na_bias_table(rpb, rows):
    nk, nq = NA_K_ROWS * GRID_W, NA_Q_ROWS * GRID_W
    drs, dcs, oks = [], [], []
    for g in (0, 1, rows // NA_Q_ROWS - 1):
        r0, w0, _ = _na_group(g, rows)
        m = np.arange(nk)
        ki, kj = w0 + m // GRID_W, m % GRID_W
        n = np.arange(nq)
        r, c = r0 + n // GRID_W, n % GRID_W
        rs = np.clip(r - NA_WIN_ROWS // 2, 0, rows - NA_WIN_ROWS)
        cs = np.clip(c - NA_WIN_COLS // 2, 0, GRID_W - NA_WIN_COLS)
        ok = ((ki[:, None] >= rs[None]) & (ki[:, None] < rs[None] + NA_WIN_ROWS)
              & (kj[:, None] >= cs[None]) & (kj[:, None] < cs[None] + NA_WIN_COLS))
        dr = np.clip(ki[:, None] - r[None] + NA_WIN_ROWS - 1, 0, 2 * NA_WIN_ROWS - 2)
        dc = np.clip(kj[:, None] - c[None] + NA_WIN_COLS - 1, 0, 2 * NA_WIN_COLS - 2)
        drs.append(dr), dcs.append(dc), oks.append(ok)
    dr, dc, ok = np.stack(drs), np.stack(dcs), np.stack(oks)
    return jnp.where(ok[None], rpb[:, dr, dc].astype(F32), NEG_BIG)


def _na_body(q_ref, k_ref, v_ref, bias_ref, o_ref, *, rows, scale):
    vt = v_ref[...].astype(F32).T.astype(BF16)
    nq, nk = NA_Q_ROWS * GRID_W, NA_K_ROWS * GRID_W
    for g in range(rows // NA_Q_ROWS):
        r0, w0, var = _na_group(g, rows)
        kwin = k_ref[w0 * GRID_W:w0 * GRID_W + nk, :]
        qg = q_ref[r0 * GRID_W:r0 * GRID_W + nq, :]
        st = lax.dot_general(kwin, qg, _NT, preferred_element_type=F32) * scale + bias_ref[0, var]
        m = jnp.max(st, axis=0, keepdims=True)
        e = jnp.exp(st - m)
        l = jnp.sum(e, axis=0, keepdims=True)
        ot = lax.dot_general(vt[:, w0 * GRID_W:w0 * GRID_W + nk], e.astype(BF16), _NN,
                             preferred_element_type=F32)
        o_ref[r0 * GRID_W:r0 * GRID_W + nq, :] = (ot / l).T.astype(o_ref.dtype)


def _na_attention(qkv, bias, *, batch, seq):
    t = qkv.shape[0]
    rows = seq // GRID_W
    nk, nq = NA_K_ROWS * GRID_W, NA_Q_ROWS * GRID_W
    scale = float(NA_HEAD_DIM ** -0.5)
    return pl.pallas_call(
        functools.partial(_na_body, rows=rows, scale=scale),
        grid=(NA_HEADS, batch),
        in_specs=[pl.BlockSpec((seq, NA_HEAD_DIM), lambda h, b: (b, h)),
                  pl.BlockSpec((seq, NA_HEAD_DIM), lambda h, b: (b, NA_HEADS + h)),
                  pl.BlockSpec((seq, NA_HEAD_DIM), lambda h, b: (b, 2 * NA_HEADS + h)),
                  pl.BlockSpec((1, 3, nk, nq), lambda h, b: (h, 0, 0, 0))],
        out_specs=pl.BlockSpec((seq, NA_HEAD_DIM), lambda h, b: (b, h)),
        out_shape=jax.ShapeDtypeStruct((t, NA_HEADS * NA_HEAD_DIM), BF16),
        compiler_params=_params(("arbitrary", "arbitrary")),
        name="na_attention",
    )(qkv, qkv, qkv, bias)


def _router_body(x_ref, g_ref, wr_ref, br_ref, hn_ref, idx_ref, wts_ref, rank_ref, cnt_ref, carry_ref,
                 *, n_exp, tr):
    i = pl.program_id(0)

    @pl.when(i == 0)
    def _():
        carry_ref[...] = jnp.zeros_like(carry_ref)

    y = _rms(x_ref[...], g_ref[...])
    yb = y.astype(BF16)
    half = y.shape[1] // 2
    lo = lax.bitcast_convert_type(yb[:, :half].astype(F32), U32)
    hi = lax.bitcast_convert_type(yb[:, half:].astype(F32), U32)
    hn_ref[...] = (hi & jnp.uint32(0xFFFF0000)) | (lo >> 16)

    logits = lax.dot_general(wr_ref[...].astype(BF16), yb, _NT, preferred_element_type=F32) + br_ref[...]
    eid = lax.broadcasted_iota(I32, (n_exp, tr), 0).astype(F32)
    work = logits
    vals, sels = [], []
    for k in range(TOP_K):
        m = jnp.max(work, axis=0, keepdims=True)
        first = jnp.min(jnp.where(work == m, eid, float(n_exp)), axis=0, keepdims=True)
        sel = eid == first
        vals.append(m)
        sels.append(sel)
        idx_ref[k:k + 1, :] = first.astype(I32)
        work = jnp.where(sel, -jnp.inf, work)
    es = [jnp.exp(v - vals[0]) for v in vals]
    denom = es[0] + es[1] + es[2] + es[3]
    for k in range(TOP_K):
        wts_ref[k:k + 1, :] = es[k] / denom

    chosen = jnp.zeros((n_exp, tr), F32)
    for sel in sels:
        chosen = chosen + sel.astype(F32)
    before = (lax.broadcasted_iota(I32, (tr, tr), 0) < lax.broadcasted_iota(I32, (tr, tr), 1)).astype(BF16)
    carry = carry_ref[:, 0:1]
    base = lax.dot_general(chosen.astype(BF16), before, _NN, preferred_element_type=F32) + carry
    for k in range(TOP_K):
        rank_ref[k:k + 1, :] = jnp.sum(jnp.where(sels[k], base, 0.0), axis=0, keepdims=True).astype(I32)
    total = carry + jnp.sum(chosen, axis=1, keepdims=True)
    carry_ref[...] = jnp.broadcast_to(total, carry_ref.shape)
    cnt_ref[...] = jnp.broadcast_to(total, cnt_ref.shape).astype(I32)


def _router(x1, g, wr_t, br, *, tr=512):
    t, d = x1.shape
    n_exp = wr_t.shape[0]
    return pl.pallas_call(
        functools.partial(_router_body, n_exp=n_exp, tr=tr),
        grid=(t // tr,),
        in_specs=[pl.BlockSpec((tr, d), lambda i: (i, 0)),
                  pl.BlockSpec((1, d), lambda i: (0, 0)),
                  pl.BlockSpec((n_exp, d), lambda i: (0, 0)),
                  pl.BlockSpec((n_exp, 1), lambda i: (0, 0))],
        out_specs=[pl.BlockSpec((tr, d // 2), lambda i: (i, 0)),
                   pl.BlockSpec((TOP_K, tr), lambda i: (0, i)),
                   pl.BlockSpec((TOP_K, tr), lambda i: (0, i)),
                   pl.BlockSpec((TOP_K, tr), lambda i: (0, i)),
                   pl.BlockSpec((n_exp, 128), lambda i: (0, 0))],
        out_shape=[jax.ShapeDtypeStruct((t, d // 2), U32),
                   jax.ShapeDtypeStruct((TOP_K, t), I32),
                   jax.ShapeDtypeStruct((TOP_K, t), F32),
                   jax.ShapeDtypeStruct((TOP_K, t), I32),
                   jax.ShapeDtypeStruct((n_exp, 128), I32)],
        scratch_shapes=[pltpu.VMEM((n_exp, 128), F32)],
        compiler_params=_params(("arbitrary",)),
        name="router_topk",
    )(x1, g.reshape(1, d), wr_t, br.reshape(n_exp, 1))


def _dispatch_body(pos_ref, hn_ref, xg_ref, sem, *, td):
    base = pl.program_id(0) * td

    def row_copy(t, k):
        return pltpu.make_async_copy(hn_ref.at[pl.ds(base + t, 1), :],
                                     xg_ref.at[pl.ds(pos_ref[k, t], 1), :], sem)

    def start(t, c):
        for k in range(TOP_K):
            row_copy(t, k).start()
        return c

    def wait(t, c):
        for k in range(TOP_K):
            row_copy(t, k).wait()
        return c

    lax.fori_loop(0, td, start, 0)
    lax.fori_loop(0, td, wait, 0)


def _dispatch(pos, hn, n_rows, *, td=512):
    t = hn.shape[0]
    return pl.pallas_call(
        functools.partial(_dispatch_body, td=td),
        grid=(t // td,),
        in_specs=[pl.BlockSpec((TOP_K, td), lambda i: (0, i), memory_space=pltpu.SMEM),
                  pl.BlockSpec(memory_space=pl.ANY)],
        out_specs=pl.BlockSpec(memory_space=pl.ANY),
        out_shape=jax.ShapeDtypeStruct((n_rows, hn.shape[1]), hn.dtype),
        scratch_shapes=[pltpu.SemaphoreType.DMA(())],
        compiler_params=_params(("arbitrary",)),
        name="moe_dispatch",
    )(pos, hn)


def _expert_up_body(ie_ref, ib_ref, ins_ref, inr_ref, x_ref, w_ref, b_ref, o_ref, *, tn):
    w = pl.program_id(0)
    nsub = ins_ref[w]
    nrows = inr_ref[w]
    half = w_ref.shape[0] // 2

    for n in range(1, ITEM_SUBS + 1):
        @pl.when(nsub == n)
        def _(n=n):
            m = n * SUB
            xw = x_ref[0:m, :]
            row = lax.broadcasted_iota(I32, (m, 1), 0)
            xw = jnp.where(row < nrows, xw, jnp.uint32(0))
            lo = lax.bitcast_convert_type(xw << 16, F32).astype(BF16)
            hi = lax.bitcast_convert_type(xw & jnp.uint32(0xFFFF0000), F32).astype(BF16)
            gu = (lax.dot_general(lo, w_ref[0:half, :].astype(BF16), _NN, preferred_element_type=F32)
                  + lax.dot_general(hi, w_ref[half:, :].astype(BF16), _NN, preferred_element_type=F32)
                  + b_ref[...])
            gate = jnp.minimum(gu, SWIGLU_LIMIT)
            up = jnp.clip(pltpu.roll(gu, tn - 1, 1), -SWIGLU_LIMIT, SWIGLU_LIMIT)
            act = (up + 1.0) * gate * jax.nn.sigmoid(SWIGLU_ALPHA * gate)
            lane = lax.broadcasted_iota(I32, act.shape, 1)
            act = jnp.where((lane & 1) == 0, act, 0.0).astype(BF16)
            pick = (lax.broadcasted_iota(I32, (tn, tn // 2), 0)
                    == 2 * lax.broadcasted_iota(I32, (tn, tn // 2), 1)).astype(BF16)
            o_ref[0:m, :] = lax.dot_general(act, pick, _NN, preferred_element_type=F32).astype(o_ref.dtype)
            if m < ITEM_ROWS:
                o_ref[m:, :] = jnp.zeros((ITEM_ROWS - m, tn // 2), o_ref.dtype)


def _expert_up(items, xg, w_gu, b_gu, *, tn=512):
    item_e, item_blk, item_nsub, item_rows = items
    n_items = item_e.shape[0]
    n_exp, d, f2 = w_gu.shape
    nj = f2 // tn
    n_rows = xg.shape[0]

    def jmap(j, ns, w):
        return jnp.where(ns[w] > 0, j, nj - 1)

    grid_spec = pltpu.PrefetchScalarGridSpec(
        num_scalar_prefetch=4,
        grid=(n_items, nj),
        in_specs=[pl.BlockSpec((ITEM_ROWS, xg.shape[1]), lambda w, j, ie, ib, ns, nr: (ib[w], 0)),
                  pl.BlockSpec((None, d, tn), lambda w, j, ie, ib, ns, nr: (ie[w], 0, jmap(j, ns, w))),
                  pl.BlockSpec((None, 1, tn), lambda w, j, ie, ib, ns, nr: (ie[w], 0, jmap(j, ns, w)))],
        out_specs=pl.BlockSpec((ITEM_ROWS, tn // 2), lambda w, j, ie, ib, ns, nr: (ib[w], jmap(j, ns, w))),
    )
    return pl.pallas_call(
        functools.partial(_expert_up_body, tn=tn),
        grid_spec=grid_spec,
        out_shape=jax.ShapeDtypeStruct((n_rows, f2 // 2), BF16),
        compiler_params=_params(("arbitrary", "arbitrary")),
        name="expert_gate_up",
    )(item_e, item_blk, item_nsub, item_rows, xg, w_gu, b_gu.reshape(n_exp, 1, f2))


def _expert_down_body(ie_ref, ib_ref, ins_ref, a_ref, w_ref, b_ref, o_ref):
    w = pl.program_id(0)
    nsub = ins_ref[w]

    for n in range(1, ITEM_SUBS + 1):
        @pl.when(nsub == n)
        def _(n=n):
            m = n * SUB
            y = lax.dot_general(a_ref[0:m, :], w_ref[...].astype(BF16), _NN, preferred_element_type=F32)
            o_ref[0:m, :] = y + b_ref[...]
            if m < ITEM_ROWS:
                o_ref[m:, :] = jnp.zeros((ITEM_ROWS - m, o_ref.shape[1]), o_ref.dtype)


def _expert_down(items, act, w_d, b_d, *, tn=512):
    item_e, item_blk, item_nsub, _ = items
    n_items = item_e.shape[0]
    n_exp, f, d = w_d.shape
    nj = d // tn
    n_rows = act.shape[0]

    def jmap(j, ns, w):
        return jnp.where(ns[w] > 0, j, nj - 1)

    grid_spec = pltpu.PrefetchScalarGridSpec(
        num_scalar_prefetch=3,
        grid=(n_items, nj),
        in_specs=[pl.BlockSpec((ITEM_ROWS, f), lambda w, j, ie, ib, ns: (ib[w], 0)),
                  pl.BlockSpec((None, f, tn), lambda w, j, ie, ib, ns: (ie[w], 0, jmap(j, ns, w))),
                  pl.BlockSpec((None, 1, tn), lambda w, j, ie, ib, ns: (ie[w], 0, jmap(j, ns, w)))],
        out_specs=pl.BlockSpec((ITEM_ROWS, tn), lambda w, j, ie, ib, ns: (ib[w], jmap(j, ns, w))),
    )
    return pl.pallas_call(
        _expert_down_body,
        grid_spec=grid_spec,
        out_shape=jax.ShapeDtypeStruct((n_rows, d), F32),
        compiler_params=_params(("arbitrary", "arbitrary")),
        name="expert_down",
    )(item_e, item_blk, item_nsub, act, w_d, b_d.reshape(n_exp, 1, d))


def _combine_body(pos_ref, posn_ref, wts_ref, x_ref, g_ref, yg_ref, o_ref, buf_ref, sem, *, tc):
    i = pl.program_id(0)
    n = pl.num_programs(0)
    slot = lax.rem(i, 2)

    def row_copy(p_ref, s, t, k):
        return pltpu.make_async_copy(yg_ref.at[pl.ds(p_ref[k, t], 1), :],
                                     buf_ref.at[s, k, pl.ds(t, 1), :], sem.at[s])

    def start_tile(p_ref, s):
        def body(t, c):
            for k in range(TOP_K):
                row_copy(p_ref, s, t, k).start()
            return c
        lax.fori_loop(0, tc, body, 0)

    @pl.when(i == 0)
    def _():
        start_tile(pos_ref, 0)

    @pl.when(i + 1 < n)
    def _():
        start_tile(posn_ref, 1 - slot)

    def wait_body(t, c):
        for k in range(TOP_K):
            row_copy(pos_ref, slot, t, k).wait()
        return c
    lax.fori_loop(0, tc, wait_body, 0)

    acc = x_ref[...]
    for k in range(TOP_K):
        acc = acc + wts_ref[:, k:k + 1] * buf_ref[slot, k]
    o_ref[...] = _rms(acc, g_ref[...])


def _combine(pos, wts_t, x1, g, yg, *, tc=128):
    t, d = x1.shape
    nt = t // tc
    return pl.pallas_call(
        functools.partial(_combine_body, tc=tc),
        grid=(nt,),
        in_specs=[pl.BlockSpec((TOP_K, tc), lambda i: (0, i), memory_space=pltpu.SMEM),
                  pl.BlockSpec((TOP_K, tc), lambda i: (0, jnp.minimum(i + 1, nt - 1)), memory_space=pltpu.SMEM),
                  pl.BlockSpec((tc, TOP_K), lambda i: (i, 0)),
                  pl.BlockSpec((tc, d), lambda i: (i, 0)),
                  pl.BlockSpec((1, d), lambda i: (0, 0)),
                  pl.BlockSpec(memory_space=pl.ANY)],
        out_specs=pl.BlockSpec((tc, d), lambda i: (i, 0)),
        out_shape=jax.ShapeDtypeStruct((t, d), F32),
        scratch_shapes=[pltpu.VMEM((2, TOP_K, tc, d), F32), pltpu.SemaphoreType.DMA((2,))],
        compiler_params=_params(("arbitrary",)),
        name="moe_combine_norm",
    )(pos, pos, wts_t, x1, g.reshape(1, d), yg)


def _plan_items(counts, n_assign):
    n_exp = counts.shape[0]
    max_items = n_assign // ITEM_ROWS + n_exp
    nsub_e = (counts + SUB - 1) // SUB
    nitem_e = (counts + ITEM_ROWS - 1) // ITEM_ROWS
    last_item_e = jnp.cumsum(nitem_e)
    first_item_e = last_item_e - nitem_e
    total = last_item_e[-1]
    w = jnp.arange(max_items, dtype=I32)
    valid = w < total
    e_w = jnp.minimum(jnp.searchsorted(last_item_e, w, side="right"), n_exp - 1).astype(I32)
    e_last = e_w[jnp.maximum(total - 1, 0)]
    e_w = jnp.where(valid, e_w, e_last)
    c_w = w - first_item_e[e_w]
    nsub_w = jnp.where(valid, jnp.clip(nsub_e[e_w] - ITEM_SUBS * c_w, 0, ITEM_SUBS), 0).astype(I32)
    rows_w = jnp.where(valid, jnp.clip(counts[e_w] - ITEM_ROWS * c_w, 0, ITEM_ROWS), 0).astype(I32)
    blk_w = jnp.where(valid, w, max_items).astype(I32)
    row_off_e = (first_item_e * ITEM_ROWS).astype(I32)
    return (e_w, blk_w, nsub_w, rows_w), row_off_e, (max_items + 1) * ITEM_ROWS


def kernel(x, positions, norm_mix, w_in, q_a_norm, w_q_b, kv_a_norm, w_kv_b, na_rpb, w_proj_a, w_proj_b, w_out, norm_ffn, w_router, b_router, w_gate_up, b_gate_up, w_down, b_down, norm_final):
    b, s, d = x.shape
    t = b * s
    na_w = NA_HEADS * NA_HEAD_DIM
    q_rank, kv_rank = q_a_norm.shape[1], kv_a_norm.shape[1]
    lat0 = 3 * na_w
    gate0 = lat0 + q_rank + kv_rank + MLA_ROPE_DIM
    xf = x.reshape(t, d)
    w_in0 = w_in[0]

    hn = _rmsnorm(xf, norm_mix[0])
    qkv = _matmul(hn, w_in0, col0=0, ncols=lat0, tm=2048, tn=512, out_dtype=BF16, name="proj_qkv")
    lat = _matmul(hn, w_in0, col0=lat0, ncols=2048, tm=2048, tn=512, out_dtype=F32, name="proj_latent")
    gates = _matmul(hn, w_in0[:, gate0:], col0=0, ncols=2 * d, tm=2048, tn=512, out_dtype=F32,
                    act="sigmoid", name="proj_gates")

    oa = _na_attention(qkv, _na_bias_table(na_rpb[0], s // GRID_W), batch=b, seq=s)

    half = MLA_ROPE_DIM // 2
    inv_freq = ROPE_THETA ** (-(jnp.arange(half, dtype=F32) * 2.0) / MLA_ROPE_DIM)
    inv_freq128 = jnp.tile(inv_freq, 128 // half).reshape(1, 128)
    cqn, ckvn, kpe, cos, sin = _mla_prep(lat, positions.reshape(t), q_a_norm[0], kv_a_norm[0], inv_freq128)
    qk_dim = MLA_NOPE_DIM + MLA_ROPE_DIM
    wq = w_q_b[0].reshape(q_rank, MLA_HEADS, qk_dim)
    wq = jnp.concatenate([wq[:, :, :MLA_NOPE_DIM].reshape(q_rank, -1),
                          wq[:, :, MLA_NOPE_DIM:].reshape(q_rank, -1)], axis=1)
    q2 = _qproj(cqn, wq, cos, sin, rope_col0=MLA_HEADS * MLA_NOPE_DIM, tm=2048, tn=512)
    kv = _matmul(ckvn, w_kv_b[0], col0=0, ncols=w_kv_b.shape[2], tm=2048, tn=1024, out_dtype=BF16,
                 name="kv_proj")
    ob = _mla_attention(q2, kv, kpe, batch=b, seq=s)

    y = _merge(oa, ob, w_proj_a[0], w_proj_b[0], gates, tm=1024, tn=512)
    x1 = _outproj(y, w_out[0], xf, tm=2048, tn=512)

    hn_packed, idx, wts, rank, counts = _router(x1, norm_ffn[0], w_router[0].T, b_router[0])
    items, row_off, n_rows = _plan_items(counts[:, 0], t * TOP_K)
    pos = row_off[idx] + rank
    xg = _dispatch(pos, hn_packed, n_rows)
    act = _expert_up(items, xg, w_gate_up[0], b_gate_up[0])
    yg = _expert_down(items, act, w_down[0], b_down[0])
    out = _combine(pos, wts.T, x1, norm_final, yg)
    return out.reshape(b, s, d)
```

```python
import functools

import numpy as np
import jax
import jax.numpy as jnp
from jax import lax
from jax.experimental import pallas as pl
from jax.experimental.pallas import tpu as pltpu

F32 = jnp.float32
BF16 = jnp.bfloat16
U32 = jnp.uint32
I32 = jnp.int32

GRID_W = 64
NA_HEADS = 16
NA_HEAD_DIM = 128
NA_WIN_ROWS = 8
NA_WIN_COLS = 16
MLA_HEADS = 16
MLA_NOPE_DIM = 128
MLA_ROPE_DIM = 64
MLA_V_DIM = 128
ROPE_THETA = 10000.0
TOP_K = 4
SWIGLU_LIMIT = 7.0
SWIGLU_ALPHA = 1.702
NORM_EPS = 1e-6

NEG_BIG = -1e30

V7X_VMEM_BYTES = 64 * 1024 * 1024
VMEM_LIMIT = V7X_VMEM_BYTES - 4 * 1024 * 1024

SUB = 256
ITEM_SUBS = 5
ITEM_ROWS = SUB * ITEM_SUBS

_NN = (((1,), (0,)), ((), ()))
_NT = (((1,), (1,)), ((), ()))


def _params(sem=None):
    return pltpu.CompilerParams(vmem_limit_bytes=VMEM_LIMIT, dimension_semantics=sem)


def _rms(x, g):
    return x * lax.rsqrt(jnp.mean(x * x, axis=-1, keepdims=True) + NORM_EPS) * g


def _rmsnorm_body(x_ref, g_ref, o_ref):
    o_ref[...] = _rms(x_ref[...], g_ref[...]).astype(o_ref.dtype)


def _rmsnorm(x, g, *, tm=512):
    t, d = x.shape
    return pl.pallas_call(
        _rmsnorm_body,
        grid=(t // tm,),
        in_specs=[pl.BlockSpec((tm, d), lambda i: (i, 0)),
                  pl.BlockSpec((1, d), lambda i: (0, 0))],
        out_specs=pl.BlockSpec((tm, d), lambda i: (i, 0)),
        out_shape=jax.ShapeDtypeStruct((t, d), BF16),
        compiler_params=_params(("arbitrary",)),
        name="rmsnorm",
    )(x, g.reshape(1, d))


def _mm_body(a_ref, w_ref, o_ref, *, act):
    acc = lax.dot_general(a_ref[...], w_ref[...].astype(BF16), _NN, preferred_element_type=F32)
    if act == "sigmoid":
        acc = jax.nn.sigmoid(acc)
    o_ref[...] = acc.astype(o_ref.dtype)


def _matmul(a, w, *, col0, ncols, tm, tn, out_dtype, act=None, name):
    t, k = a.shape
    return pl.pallas_call(
        functools.partial(_mm_body, act=act),
        grid=(t // tm, ncols // tn),
        in_specs=[pl.BlockSpec((tm, k), lambda i, j: (i, 0), pipeline_mode=pl.Buffered(1)),
                  pl.BlockSpec((k, tn), lambda i, j: (0, j + col0 // tn))],
        out_specs=pl.BlockSpec((tm, tn), lambda i, j: (i, j)),
        out_shape=jax.ShapeDtypeStruct((t, ncols), out_dtype),
        compiler_params=_params(("arbitrary", "arbitrary")),
        name=name,
    )(a, w)


def _qproj_body(a_ref, w_ref, cos_ref, sin_ref, o_ref, *, first_rope_block, tn):
    acc = lax.dot_general(a_ref[...], w_ref[...].astype(BF16), _NN, preferred_element_type=F32)
    j = pl.program_id(1)

    @pl.when(j < first_rope_block)
    def _():
        o_ref[...] = acc.astype(o_ref.dtype)

    @pl.when(j >= first_rope_block)
    def _():
        reps = tn // 128
        c = jnp.concatenate([cos_ref[...]] * reps, axis=1)
        s = jnp.concatenate([sin_ref[...]] * reps, axis=1)
        lane = lax.broadcasted_iota(I32, acc.shape, 1)
        first = (lane & (MLA_ROPE_DIM - 1)) < MLA_ROPE_DIM // 2
        half = MLA_ROPE_DIM // 2
        partner = jnp.where(first, pltpu.roll(acc, tn - half, 1), pltpu.roll(acc, half, 1))
        o_ref[...] = (acc * c + partner * s).astype(o_ref.dtype)


def _qproj(a, w, cos, sin, *, rope_col0, tm, tn):
    t, k = a.shape
    n = w.shape[1]
    return pl.pallas_call(
        functools.partial(_qproj_body, first_rope_block=rope_col0 // tn, tn=tn),
        grid=(t // tm, n // tn),
        in_specs=[pl.BlockSpec((tm, k), lambda i, j: (i, 0), pipeline_mode=pl.Buffered(1)),
                  pl.BlockSpec((k, tn), lambda i, j: (0, j)),
                  pl.BlockSpec((tm, 128), lambda i, j: (i, 0)),
                  pl.BlockSpec((tm, 128), lambda i, j: (i, 0))],
        out_specs=pl.BlockSpec((tm, tn), lambda i, j: (i, j)),
        out_shape=jax.ShapeDtypeStruct((t, n), BF16),
        compiler_params=_params(("arbitrary", "arbitrary")),
        name="q_proj_rope",
    )(a, w, cos, sin)


def _merge_body(oa_ref, ob_ref, wa_ref, wb_ref, ga_ref, gb_ref, o_ref):
    pa = lax.dot_general(oa_ref[...], wa_ref[...].astype(BF16), _NN, preferred_element_type=F32)
    pb = lax.dot_general(ob_ref[...], wb_ref[...].astype(BF16), _NN, preferred_element_type=F32)
    o_ref[...] = (ga_ref[...] * pa + gb_ref[...] * pb).astype(o_ref.dtype)


def _merge(oa, ob, wa, wb, gates, *, tm, tn):
    t, k = oa.shape
    n = wa.shape[1]
    nb = n // tn
    return pl.pallas_call(
        _merge_body,
        grid=(t // tm, nb),
        in_specs=[pl.BlockSpec((tm, k), lambda i, j: (i, 0), pipeline_mode=pl.Buffered(1)),
                  pl.BlockSpec((tm, k), lambda i, j: (i, 0), pipeline_mode=pl.Buffered(1)),
                  pl.BlockSpec((k, tn), lambda i, j: (0, j)),
                  pl.BlockSpec((k, tn), lambda i, j: (0, j)),
                  pl.BlockSpec((tm, tn), lambda i, j: (i, j)),
                  pl.BlockSpec((tm, tn), lambda i, j: (i, j + nb))],
        out_specs=pl.BlockSpec((tm, tn), lambda i, j: (i, j)),
        out_shape=jax.ShapeDtypeStruct((t, n), BF16),
        compiler_params=_params(("arbitrary", "arbitrary")),
        name="gated_merge",
    )(oa, ob, wa, wb, gates, gates)


def _outproj_body(a_ref, w_ref, r_ref, o_ref):
    acc = lax.dot_general(a_ref[...], w_ref[...].astype(BF16), _NN, preferred_element_type=F32)
    o_ref[...] = r_ref[...] + acc


def _outproj(a, w, resid, *, tm, tn):
    t, k = a.shape
    n = w.shape[1]
    return pl.pallas_call(
        _outproj_body,
        grid=(t // tm, n // tn),
        in_specs=[pl.BlockSpec((tm, k), lambda i, j: (i, 0), pipeline_mode=pl.Buffered(1)),
                  pl.BlockSpec((k, tn), lambda i, j: (0, j)),
                  pl.BlockSpec((tm, tn), lambda i, j: (i, j))],
        out_specs=pl.BlockSpec((tm, tn), lambda i, j: (i, j)),
        out_shape=jax.ShapeDtypeStruct((t, n), F32),
        compiler_params=_params(("arbitrary", "arbitrary")),
        name="out_proj_residual",
    )(a, w, resid)


def _mla_prep_body(lat_ref, pos_ref, qn_ref, kvn_ref, invf_ref,
                   cq_ref, ckv_ref, kpe_ref, cos_ref, sin_ref, *, q_rank, kv_rank):
    cq_ref[...] = _rms(lat_ref[:, 0:q_rank], qn_ref[...]).astype(cq_ref.dtype)
    ckv_ref[...] = _rms(lat_ref[:, q_rank:q_rank + kv_rank], kvn_ref[...]).astype(ckv_ref.dtype)
    kr = lat_ref[:, q_rank + kv_rank:q_rank + kv_rank + 128]
    ang = pos_ref[...].astype(F32) * invf_ref[...]
    c = jnp.cos(ang)
    s = jnp.sin(ang)
    lane = lax.broadcasted_iota(I32, ang.shape, 1)
    half = MLA_ROPE_DIM // 2
    first = (lane & (MLA_ROPE_DIM - 1)) < half
    s = jnp.where(first, -s, s)
    partner = jnp.where(first, pltpu.roll(kr, 128 - half, 1), pltpu.roll(kr, half, 1))
    kpe = jnp.where(lane < MLA_ROPE_DIM, kr * c + partner * s, 0.0)
    kpe_ref[:, 0:128] = kpe.astype(kpe_ref.dtype)
    kpe_ref[:, 128:256] = pltpu.roll(kpe, MLA_ROPE_DIM, 1).astype(kpe_ref.dtype)
    cos_ref[...] = c
    sin_ref[...] = s


def _mla_prep(lat, positions, q_norm, kv_norm, inv_freq128, *, tm=512):
    t = lat.shape[0]
    q_rank, kv_rank = q_norm.shape[0], kv_norm.shape[0]
    return pl.pallas_call(
        functools.partial(_mla_prep_body, q_rank=q_rank, kv_rank=kv_rank),
        grid=(t // tm,),
        in_specs=[pl.BlockSpec((tm, lat.shape[1]), lambda i: (i, 0)),
                  pl.BlockSpec((tm, 1), lambda i: (i, 0)),
                  pl.BlockSpec((1, q_rank), lambda i: (0, 0)),
                  pl.BlockSpec((1, kv_rank), lambda i: (0, 0)),
                  pl.BlockSpec((1, 128), lambda i: (0, 0))],
        out_specs=[pl.BlockSpec((tm, q_rank), lambda i: (i, 0)),
                   pl.BlockSpec((tm, kv_rank), lambda i: (i, 0)),
                   pl.BlockSpec((tm, 256), lambda i: (i, 0)),
                   pl.BlockSpec((tm, 128), lambda i: (i, 0)),
                   pl.BlockSpec((tm, 128), lambda i: (i, 0))],
        out_shape=[jax.ShapeDtypeStruct((t, q_rank), BF16),
                   jax.ShapeDtypeStruct((t, kv_rank), BF16),
                   jax.ShapeDtypeStruct((t, 256), BF16),
                   jax.ShapeDtypeStruct((t, 128), F32),
                   jax.ShapeDtypeStruct((t, 128), F32)],
        compiler_params=_params(("arbitrary",)),
        name="mla_prep",
    )(lat, positions.reshape(t, 1), q_norm.reshape(1, -1), kv_norm.reshape(1, -1), inv_freq128)


def _mla_body(qn_ref, qp_ref, kv_ref, kpe_ref, o_ref, kf_ref, vt_ref, *, scale):
    qi = pl.program_id(2)

    @pl.when(qi == 0)
    def _():
        for hh in range(2):
            kf_ref[hh, :, 0:128] = kv_ref[:, 256 * hh:256 * hh + 128]
            kf_ref[hh, :, 128:256] = kpe_ref[:, 128 * hh:128 * hh + 128]
            v = kv_ref[:, 256 * hh + 128:256 * hh + 256]
            vt_ref[hh] = v.astype(F32).T.astype(BF16)

    lane = lax.broadcasted_iota(I32, qp_ref.shape, 1)
    for hh in range(2):
        qp = qp_ref[...]
        keep = (lane >= MLA_ROPE_DIM) if hh else (lane < MLA_ROPE_DIM)
        qp = jnp.where(keep, qp, jnp.zeros_like(qp))
        qf = jnp.concatenate([qn_ref[:, 128 * hh:128 * hh + 128], qp], axis=1)
        st = lax.dot_general(kf_ref[hh], qf, _NT, preferred_element_type=F32) * scale
        m = jnp.max(st, axis=0, keepdims=True)
        e = jnp.exp(st - m)
        l = jnp.sum(e, axis=0, keepdims=True)
        ot = lax.dot_general(vt_ref[hh], e.astype(BF16), _NN, preferred_element_type=F32)
        o_ref[:, 128 * hh:128 * hh + 128] = (ot / l).T.astype(o_ref.dtype)


def _mla_attention(q2, kv, kpe, *, batch, seq, tq=256):
    t = q2.shape[0]
    nq = seq // tq
    hp = MLA_HEADS // 2
    nope_w = MLA_HEADS * MLA_NOPE_DIM
    scale = float((MLA_NOPE_DIM + MLA_ROPE_DIM) ** -0.5)
    return pl.pallas_call(
        functools.partial(_mla_body, scale=scale),
        grid=(batch, hp, nq),
        in_specs=[pl.BlockSpec((tq, 256), lambda b, h, q: (b * nq + q, h)),
                  pl.BlockSpec((tq, 128), lambda b, h, q: (b * nq + q, nope_w // 128 + h)),
                  pl.BlockSpec((seq, 512), lambda b, h, q: (b, h)),
                  pl.BlockSpec((seq, 256), lambda b, h, q: (b, 0))],
        out_specs=pl.BlockSpec((tq, 256), lambda b, h, q: (b * nq + q, h)),
        out_shape=jax.ShapeDtypeStruct((t, MLA_HEADS * MLA_V_DIM), BF16),
        scratch_shapes=[pltpu.VMEM((2, seq, 256), BF16), pltpu.VMEM((2, 128, seq), BF16)],
        compiler_params=_params(("arbitrary", "arbitrary", "arbitrary")),
        name="mla_attention",
    )(q2, q2, kv, kpe)


NA_Q_ROWS = 4
NA_K_ROWS = 12


def _na_group(g, rows):
    r0 = NA_Q_ROWS * g
    w0 = min(max(r0 - NA_WIN_ROWS // 2, 0), rows - NA_K_ROWS)
    if g == 0:
        var = 0
    elif g == rows // NA_Q_ROWS - 1:
        var = 2
    else:
        var = 1
    return r0, w0, var


def _na_bias_table(rpb, rows):
    n_heads = rpb.shape[0]
    kj = np.arange(GRID_W)[:, None]
    c = np.arange(GRID_W)[None, :]
    cs = np.clip(c - NA_WIN_COLS // 2, 0, GRID_W - NA_WIN_COLS)
    col_ok = (kj >= cs) & (kj < cs + NA_WIN_COLS)
    dc = kj - c + NA_WIN_COLS - 1
    pick = np.stack([(dc == dd) & col_ok for dd in range(2 * NA_WIN_COLS - 1)]).astype(np.float32)
    toep = jnp.einsum("hrd,dkc->hrkc", rpb.astype(F32), pick, precision=lax.Precision.HIGHEST)
    toep = jnp.where(col_ok[None, None], toep, NEG_BIG)
    masked = jnp.full((n_heads, GRID_W, GRID_W), NEG_BIG, F32)
    blocks = []
    for g in (0, 1, rows // NA_Q_ROWS - 1):
        r0, w0, _ = _na_group(g, rows)
        for ki in range(w0, w0 + NA_K_ROWS):
            for r in range(r0, r0 + NA_Q_ROWS):
                rs = min(max(r - NA_WIN_ROWS // 2, 0), rows - NA_WIN_ROWS)
                ok = rs <= ki < rs + NA_WIN_ROWS
                blocks.append(toep[:, ki - r + NA_WIN_ROWS - 1] if ok else masked)
    table = jnp.stack(blocks, axis=1).reshape(n_heads, 3, NA_K_ROWS, NA_Q_ROWS, GRID_W, GRID_W)
    table = table.transpose(0, 1, 2, 4, 3, 5)
    return table.reshape(n_heads, 3, NA_K_ROWS * GRID_W, NA_Q_ROWS * GRID_W)


def _na_body(q_ref, k_ref, v_ref, bias_ref, o_ref, *, rows, scale):
    vt = v_ref[...].astype(F32).T.astype(BF16)
    nq, nk = NA_Q_ROWS * GRID_W, NA_K_ROWS * GRID_W
    for g in range(rows // NA_Q_ROWS):
        r0, w0, var = _na_group(g, rows)
        kwin = k_ref[w0 * GRID_W:w0 * GRID_W + nk, :]
        qg = q_ref[r0 * GRID_W:r0 * GRID_W + nq, :]
        st = lax.dot_general(kwin, qg, _NT, preferred_element_type=F32) * scale + bias_ref[0, var]
        m = jnp.max(st, axis=0, keepdims=True)
        e = jnp.exp(st - m)
        l = jnp.sum(e, axis=0, keepdims=True)
        ot = lax.dot_general(vt[:, w0 * GRID_W:w0 * GRID_W + nk], e.astype(BF16), _NN,
                             preferred_element_type=F32)
        o_ref[r0 * GRID_W:r0 * GRID_W + nq, :] = (ot / l).T.astype(o_ref.dtype)


def _na_attention(qkv, bias, *, batch, seq):
    t = qkv.shape[0]
    rows = seq // GRID_W
    nk, nq = NA_K_ROWS * GRID_W, NA_Q_ROWS * GRID_W
    scale = float(NA_HEAD_DIM ** -0.5)
    return pl.pallas_call(
        functools.partial(_na_body, rows=rows, scale=scale),
        grid=(NA_HEADS, batch),
        in_specs=[pl.BlockSpec((seq, NA_HEAD_DIM), lambda h, b: (b, h)),
                  pl.BlockSpec((seq, NA_HEAD_DIM), lambda h, b: (b, NA_HEADS + h)),
                  pl.BlockSpec((seq, NA_HEAD_DIM), lambda h, b: (b, 2 * NA_HEADS + h)),
                  pl.BlockSpec((1, 3, nk, nq), lambda h, b: (h, 0, 0, 0))],
        out_specs=pl.BlockSpec((seq, NA_HEAD_DIM), lambda h, b: (b, h)),
        out_shape=jax.ShapeDtypeStruct((t, NA_HEADS * NA_HEAD_DIM), BF16),
        compiler_params=_params(("arbitrary", "arbitrary")),
        name="na_attention",
    )(qkv, qkv, qkv, bias)


def _router_body(x_ref, g_ref, wr_ref, br_ref, hn_ref, idx_ref, wts_ref, rank_ref, cnt_ref, carry_ref,
                 *, n_exp, tr):
    i = pl.program_id(0)

    @pl.when(i == 0)
    def _():
        carry_ref[...] = jnp.zeros_like(carry_ref)

    y = _rms(x_ref[...], g_ref[...])
    yb = y.astype(BF16)
    half = y.shape[1] // 2
    lo = lax.bitcast_convert_type(yb[:, :half].astype(F32), U32)
    hi = lax.bitcast_convert_type(yb[:, half:].astype(F32), U32)
    hn_ref[...] = (hi & jnp.uint32(0xFFFF0000)) | (lo >> 16)

    logits = lax.dot_general(wr_ref[...].astype(BF16), yb, _NT, preferred_element_type=F32) + br_ref[...]
    eid = lax.broadcasted_iota(I32, (n_exp, tr), 0).astype(F32)
    work = logits
    vals, sels = [], []
    for k in range(TOP_K):
        m = jnp.max(work, axis=0, keepdims=True)
        first = jnp.min(jnp.where(work == m, eid, float(n_exp)), axis=0, keepdims=True)
        sel = eid == first
        vals.append(m)
        sels.append(sel)
        idx_ref[k:k + 1, :] = first.astype(I32)
        work = jnp.where(sel, -jnp.inf, work)
    es = [jnp.exp(v - vals[0]) for v in vals]
    denom = es[0] + es[1] + es[2] + es[3]
    for k in range(TOP_K):
        wts_ref[k:k + 1, :] = es[k] / denom

    chosen = jnp.zeros((n_exp, tr), F32)
    for sel in sels:
        chosen = chosen + sel.astype(F32)
    before = (lax.broadcasted_iota(I32, (tr, tr), 0) < lax.broadcasted_iota(I32, (tr, tr), 1)).astype(BF16)
    carry = carry_ref[:, 0:1]
    base = lax.dot_general(chosen.astype(BF16), before, _NN, preferred_element_type=F32) + carry
    for k in range(TOP_K):
        rank_ref[k:k + 1, :] = jnp.sum(jnp.where(sels[k], base, 0.0), axis=0, keepdims=True).astype(I32)
    total = carry + jnp.sum(chosen, axis=1, keepdims=True)
    carry_ref[...] = jnp.broadcast_to(total, carry_ref.shape)
    cnt_ref[...] = jnp.broadcast_to(total, cnt_ref.shape).astype(I32)


def _router(x1, g, wr_t, br, *, tr=512):
    t, d = x1.shape
    n_exp = wr_t.shape[0]
    return pl.pallas_call(
        functools.partial(_router_body, n_exp=n_exp, tr=tr),
        grid=(t // tr,),
        in_specs=[pl.BlockSpec((tr, d), lambda i: (i, 0)),
                  pl.BlockSpec((1, d), lambda i: (0, 0)),
                  pl.BlockSpec((n_exp, d), lambda i: (0, 0)),
                  pl.BlockSpec((n_exp, 1), lambda i: (0, 0))],
        out_specs=[pl.BlockSpec((tr, d // 2), lambda i: (i, 0)),
                   pl.BlockSpec((TOP_K, tr), lambda i: (0, i)),
                   pl.BlockSpec((TOP_K, tr), lambda i: (0, i)),
                   pl.BlockSpec((TOP_K, tr), lambda i: (0, i)),
                   pl.BlockSpec((n_exp, 128), lambda i: (0, 0))],
        out_shape=[jax.ShapeDtypeStruct((t, d // 2), U32),
                   jax.ShapeDtypeStruct((TOP_K, t), I32),
                   jax.ShapeDtypeStruct((TOP_K, t), F32),
                   jax.ShapeDtypeStruct((TOP_K, t), I32),
                   jax.ShapeDtypeStruct((n_exp, 128), I32)],
        scratch_shapes=[pltpu.VMEM((n_exp, 128), F32)],
        compiler_params=_params(("arbitrary",)),
        name="router_topk",
    )(x1, g.reshape(1, d), wr_t, br.reshape(n_exp, 1))


def _dispatch_body(pos_ref, hn_ref, xg_ref, sem, *, td):
    def row_copy(t, k):
        return pltpu.make_async_copy(hn_ref.at[pl.ds(t, 1), :],
                                     xg_ref.at[pl.ds(pos_ref[k, t], 1), :], sem)

    def start(t, c):
        for k in range(TOP_K):
            row_copy(t, k).start()
        return c

    def wait(t, c):
        for k in range(TOP_K):
            row_copy(t, k).wait()
        return c

    lax.fori_loop(0, td, start, 0)
    lax.fori_loop(0, td, wait, 0)


def _dispatch(pos, hn, n_rows, *, td=512):
    t = hn.shape[0]
    return pl.pallas_call(
        functools.partial(_dispatch_body, td=td),
        grid=(t // td,),
        in_specs=[pl.BlockSpec((TOP_K, td), lambda i: (0, i), memory_space=pltpu.SMEM),
                  pl.BlockSpec((td, hn.shape[1]), lambda i: (i, 0))],
        out_specs=pl.BlockSpec(memory_space=pl.ANY),
        out_shape=jax.ShapeDtypeStruct((n_rows, hn.shape[1]), hn.dtype),
        scratch_shapes=[pltpu.SemaphoreType.DMA(())],
        compiler_params=_params(("arbitrary",)),
        name="moe_dispatch",
    )(pos, hn)


def _expert_up_body(ie_ref, ib_ref, ins_ref, inr_ref, x_ref, w_ref, b_ref, o_ref, *, tn):
    w = pl.program_id(0)
    nsub = ins_ref[w]
    nrows = inr_ref[w]
    half = w_ref.shape[0] // 2

    for n in range(1, ITEM_SUBS + 1):
        @pl.when(nsub == n)
        def _(n=n):
            m = n * SUB
            xw = x_ref[0:m, :]
            row = lax.broadcasted_iota(I32, (m, 1), 0)
            xw = jnp.where(row < nrows, xw, jnp.uint32(0))
            lo = lax.bitcast_convert_type(xw << 16, F32).astype(BF16)
            hi = lax.bitcast_convert_type(xw & jnp.uint32(0xFFFF0000), F32).astype(BF16)
            gu = (lax.dot_general(lo, w_ref[0:half, :].astype(BF16), _NN, preferred_element_type=F32)
                  + lax.dot_general(hi, w_ref[half:, :].astype(BF16), _NN, preferred_element_type=F32)
                  + b_ref[...])
            gate = jnp.minimum(gu, SWIGLU_LIMIT)
            up = jnp.clip(pltpu.roll(gu, tn - 1, 1), -SWIGLU_LIMIT, SWIGLU_LIMIT)
            act = (up + 1.0) * gate * jax.nn.sigmoid(SWIGLU_ALPHA * gate)
            lane = lax.broadcasted_iota(I32, act.shape, 1)
            act = jnp.where((lane & 1) == 0, act, 0.0).astype(BF16)
            pick = (lax.broadcasted_iota(I32, (tn, tn // 2), 0)
                    == 2 * lax.broadcasted_iota(I32, (tn, tn // 2), 1)).astype(BF16)
            o_ref[0:m, :] = lax.dot_general(act, pick, _NN, preferred_element_type=F32).astype(o_ref.dtype)
            if m < ITEM_ROWS:
                o_ref[m:, :] = jnp.zeros((ITEM_ROWS - m, tn // 2), o_ref.dtype)


def _expert_up(items, xg, w_gu, b_gu, *, tn=512):
    item_e, item_blk, item_nsub, item_rows = items
    n_items = item_e.shape[0]
    n_exp, d, f2 = w_gu.shape
    nj = f2 // tn
    n_rows = xg.shape[0]

    def jmap(j, ns, w):
        return jnp.where(ns[w] > 0, j, nj - 1)

    grid_spec = pltpu.PrefetchScalarGridSpec(
        num_scalar_prefetch=4,
        grid=(n_items, nj),
        in_specs=[pl.BlockSpec((ITEM_ROWS, xg.shape[1]), lambda w, j, ie, ib, ns, nr: (ib[w], 0)),
                  pl.BlockSpec((None, d, tn), lambda w, j, ie, ib, ns, nr: (ie[w], 0, jmap(j, ns, w))),
                  pl.BlockSpec((None, 1, tn), lambda w, j, ie, ib, ns, nr: (ie[w], 0, jmap(j, ns, w)))],
        out_specs=pl.BlockSpec((ITEM_ROWS, tn // 2), lambda w, j, ie, ib, ns, nr: (ib[w], jmap(j, ns, w))),
    )
    return pl.pallas_call(
        functools.partial(_expert_up_body, tn=tn),
        grid_spec=grid_spec,
        out_shape=jax.ShapeDtypeStruct((n_rows, f2 // 2), BF16),
        compiler_params=_params(("arbitrary", "arbitrary")),
        name="expert_gate_up",
    )(item_e, item_blk, item_nsub, item_rows, xg, w_gu, b_gu.reshape(n_exp, 1, f2))


def _expert_down_body(ie_ref, ib_ref, ins_ref, a_ref, w_ref, b_ref, o_ref):
    w = pl.program_id(0)
    nsub = ins_ref[w]

    for n in range(1, ITEM_SUBS + 1):
        @pl.when(nsub == n)
        def _(n=n):
            m = n * SUB
            y = lax.dot_general(a_ref[0:m, :], w_ref[...].astype(BF16), _NN, preferred_element_type=F32)
            o_ref[0:m, :] = y + b_ref[...]
            if m < ITEM_ROWS:
                o_ref[m:, :] = jnp.zeros((ITEM_ROWS - m, o_ref.shape[1]), o_ref.dtype)


def _expert_down(items, act, w_d, b_d, *, tn=512):
    item_e, item_blk, item_nsub, _ = items
    n_items = item_e.shape[0]
    n_exp, f, d = w_d.shape
    nj = d // tn
    n_rows = act.shape[0]

    def jmap(j, ns, w):
        return jnp.where(ns[w] > 0, j, nj - 1)

    grid_spec = pltpu.PrefetchScalarGridSpec(
        num_scalar_prefetch=3,
        grid=(n_items, nj),
        in_specs=[pl.BlockSpec((ITEM_ROWS, f), lambda w, j, ie, ib, ns: (ib[w], 0)),
                  pl.BlockSpec((None, f, tn), lambda w, j, ie, ib, ns: (ie[w], 0, jmap(j, ns, w))),
                  pl.BlockSpec((None, 1, tn), lambda w, j, ie, ib, ns: (ie[w], 0, jmap(j, ns, w)))],
        out_specs=pl.BlockSpec((ITEM_ROWS, tn), lambda w, j, ie, ib, ns: (ib[w], jmap(j, ns, w))),
    )
    return pl.pallas_call(
        _expert_down_body,
        grid_spec=grid_spec,
        out_shape=jax.ShapeDtypeStruct((n_rows, d), F32),
        compiler_params=_params(("arbitrary", "arbitrary")),
        name="expert_down",
    )(item_e, item_blk, item_nsub, act, w_d, b_d.reshape(n_exp, 1, d))


def _combine_body(pos_ref, posn_ref, wts_ref, x_ref, g_ref, yg_ref, o_ref, buf_ref, sem, *, tc):
    i = pl.program_id(0)
    n = pl.num_programs(0)
    slot = lax.rem(i, 2)

    def row_copy(p_ref, s, t, k):
        return pltpu.make_async_copy(yg_ref.at[pl.ds(p_ref[k, t], 1), :],
                                     buf_ref.at[s, k, pl.ds(t, 1), :], sem.at[s])

    def start_tile(p_ref, s):
        def body(t, c):
            for k in range(TOP_K):
                row_copy(p_ref, s, t, k).start()
            return c
        lax.fori_loop(0, tc, body, 0)

    @pl.when(i == 0)
    def _():
        start_tile(pos_ref, 0)

    @pl.when(i + 1 < n)
    def _():
        start_tile(posn_ref, 1 - slot)

    def wait_body(t, c):
        for k in range(TOP_K):
            row_copy(pos_ref, slot, t, k).wait()
        return c
    lax.fori_loop(0, tc, wait_body, 0)

    acc = x_ref[...]
    for k in range(TOP_K):
        acc = acc + wts_ref[:, k:k + 1] * buf_ref[slot, k]
    o_ref[...] = _rms(acc, g_ref[...])


def _combine(pos, wts_t, x1, g, yg, *, tc=128):
    t, d = x1.shape
    nt = t // tc
    return pl.pallas_call(
        functools.partial(_combine_body, tc=tc),
        grid=(nt,),
        in_specs=[pl.BlockSpec((TOP_K, tc), lambda i: (0, i), memory_space=pltpu.SMEM),
                  pl.BlockSpec((TOP_K, tc), lambda i: (0, jnp.minimum(i + 1, nt - 1)), memory_space=pltpu.SMEM),
                  pl.BlockSpec((tc, TOP_K), lambda i: (i, 0)),
                  pl.BlockSpec((tc, d), lambda i: (i, 0)),
                  pl.BlockSpec((1, d), lambda i: (0, 0)),
                  pl.BlockSpec(memory_space=pl.ANY)],
        out_specs=pl.BlockSpec((tc, d), lambda i: (i, 0)),
        out_shape=jax.ShapeDtypeStruct((t, d), F32),
        scratch_shapes=[pltpu.VMEM((2, TOP_K, tc, d), F32), pltpu.SemaphoreType.DMA((2,))],
        compiler_params=_params(("arbitrary",)),
        name="moe_combine_norm",
    )(pos, pos, wts_t, x1, g.reshape(1, d), yg)


def _plan_items(counts, n_assign):
    n_exp = counts.shape[0]
    max_items = n_assign // ITEM_ROWS + n_exp
    nsub_e = (counts + SUB - 1) // SUB
    nitem_e = (counts + ITEM_ROWS - 1) // ITEM_ROWS
    last_item_e = jnp.cumsum(nitem_e)
    first_item_e = last_item_e - nitem_e
    total = last_item_e[-1]
    w = jnp.arange(max_items, dtype=I32)
    valid = w < total
    e_w = jnp.minimum(jnp.searchsorted(last_item_e, w, side="right"), n_exp - 1).astype(I32)
    e_last = e_w[jnp.maximum(total - 1, 0)]
    e_w = jnp.where(valid, e_w, e_last)
    c_w = w - first_item_e[e_w]
    nsub_w = jnp.where(valid, jnp.clip(nsub_e[e_w] - ITEM_SUBS * c_w, 0, ITEM_SUBS), 0).astype(I32)
    rows_w = jnp.where(valid, jnp.clip(counts[e_w] - ITEM_ROWS * c_w, 0, ITEM_ROWS), 0).astype(I32)
    blk_w = jnp.where(valid, w, max_items).astype(I32)
    row_off_e = (first_item_e * ITEM_ROWS).astype(I32)
    return (e_w, blk_w, nsub_w, rows_w), row_off_e, (max_items + 1) * ITEM_ROWS


def kernel(x, positions, norm_mix, w_in, q_a_norm, w_q_b, kv_a_norm, w_kv_b, na_rpb, w_proj_a, w_proj_b, w_out, norm_ffn, w_router, b_router, w_gate_up, b_gate_up, w_down, b_down, norm_final):
    b, s, d = x.shape
    t = b * s
    na_w = NA_HEADS * NA_HEAD_DIM
    q_rank, kv_rank = q_a_norm.shape[1], kv_a_norm.shape[1]
    lat0 = 3 * na_w
    gate0 = lat0 + q_rank + kv_rank + MLA_ROPE_DIM
    xf = x.reshape(t, d)
    w_in0 = w_in[0]

    hn = _rmsnorm(xf, norm_mix[0])
    qkv = _matmul(hn, w_in0, col0=0, ncols=lat0, tm=2048, tn=512, out_dtype=BF16, name="proj_qkv")
    lat = _matmul(hn, w_in0, col0=lat0, ncols=2048, tm=2048, tn=512, out_dtype=F32, name="proj_latent")
    gates = _matmul(hn, w_in0[:, gate0:], col0=0, ncols=2 * d, tm=2048, tn=512, out_dtype=F32,
                    act="sigmoid", name="proj_gates")

    oa = _na_attention(qkv, _na_bias_table(na_rpb[0], s // GRID_W), batch=b, seq=s)

    half = MLA_ROPE_DIM // 2
    inv_freq = ROPE_THETA ** (-(jnp.arange(half, dtype=F32) * 2.0) / MLA_ROPE_DIM)
    inv_freq128 = jnp.tile(inv_freq, 128 // half).reshape(1, 128)
    cqn, ckvn, kpe, cos, sin = _mla_prep(lat, positions.reshape(t), q_a_norm[0], kv_a_norm[0], inv_freq128)
    qk_dim = MLA_NOPE_DIM + MLA_ROPE_DIM
    wq = w_q_b[0].reshape(q_rank, MLA_HEADS, qk_dim)
    wq = jnp.concatenate([wq[:, :, :MLA_NOPE_DIM].reshape(q_rank, -1),
                          wq[:, :, MLA_NOPE_DIM:].reshape(q_rank, -1)], axis=1)
    q2 = _qproj(cqn, wq, cos, sin, rope_col0=MLA_HEADS * MLA_NOPE_DIM, tm=2048, tn=512)
    kv = _matmul(ckvn, w_kv_b[0], col0=0, ncols=w_kv_b.shape[2], tm=2048, tn=1024, out_dtype=BF16,
                 name="kv_proj")
    ob = _mla_attention(q2, kv, kpe, batch=b, seq=s)

    y = _merge(oa, ob, w_proj_a[0], w_proj_b[0], gates, tm=1024, tn=512)
    x1 = _outproj(y, w_out[0], xf, tm=2048, tn=512)

    hn_packed, idx, wts, rank, counts = _router(x1, norm_ffn[0], w_router[0].T, b_router[0])
    items, row_off, n_rows = _plan_items(counts[:, 0], t * TOP_K)
    onehot = idx[None] == jnp.arange(row_off.shape[0], dtype=I32)[:, None, None]
    pos = jnp.sum(jnp.where(onehot, row_off[:, None, None], 0), axis=0) + rank
    xg = _dispatch(pos, hn_packed, n_rows)
    act = _expert_up(items, xg, w_gate_up[0], b_gate_up[0])
    yg = _expert_down(items, act, w_down[0], b_down[0])
    out = _combine(pos, wts.T, x1, norm_final, yg)
    return out.reshape(b, s, d)
```

```python
import functools

import numpy as np
import jax
import jax.numpy as jnp
from jax import lax
from jax.experimental import pallas as pl
from jax.experimental.pallas import tpu as pltpu

F32 = jnp.float32
BF16 = jnp.bfloat16
U32 = jnp.uint32
I32 = jnp.int32

GRID_W = 64
NA_HEADS = 16
NA_HEAD_DIM = 128
NA_WIN_ROWS = 8
NA_WIN_COLS = 16
MLA_HEADS = 16
MLA_NOPE_DIM = 128
MLA_ROPE_DIM = 64
MLA_V_DIM = 128
ROPE_THETA = 10000.0
TOP_K = 4
SWIGLU_LIMIT = 7.0
SWIGLU_ALPHA = 1.702
NORM_EPS = 1e-6

NEG_BIG = -1e30
LOG2_E = 1.4426950408889634

V7X_VMEM_BYTES = 64 * 1024 * 1024
VMEM_LIMIT = V7X_VMEM_BYTES - 4 * 1024 * 1024

SUB = 256
ITEM_SUBS = 5
ITEM_ROWS = SUB * ITEM_SUBS

_NN = (((1,), (0,)), ((), ()))
_NT = (((1,), (1,)), ((), ()))


def _params(sem=None):
    return pltpu.CompilerParams(vmem_limit_bytes=VMEM_LIMIT, dimension_semantics=sem)


def _rms(x, g):
    return x * lax.rsqrt(jnp.mean(x * x, axis=-1, keepdims=True) + NORM_EPS) * g


def _pack_bf16_pairs(x):
    n = x.shape[1] // 2
    lo = lax.bitcast_convert_type(x[:, :n].astype(BF16).astype(F32), U32)
    hi = lax.bitcast_convert_type(x[:, n:].astype(BF16).astype(F32), U32)
    return (hi & jnp.uint32(0xFFFF0000)) | (lo >> 16)


def _unpack_bf16_pairs(w):
    return (lax.bitcast_convert_type(w << 16, F32),
            lax.bitcast_convert_type(w & jnp.uint32(0xFFFF0000), F32))


def _rmsnorm_body(x_ref, g_ref, o_ref):
    o_ref[...] = _rms(x_ref[...], g_ref[...]).astype(o_ref.dtype)


def _rmsnorm(x, g, *, tm=512):
    t, d = x.shape
    return pl.pallas_call(
        _rmsnorm_body,
        grid=(t // tm,),
        in_specs=[pl.BlockSpec((tm, d), lambda i: (i, 0)),
                  pl.BlockSpec((1, d), lambda i: (0, 0))],
        out_specs=pl.BlockSpec((tm, d), lambda i: (i, 0)),
        out_shape=jax.ShapeDtypeStruct((t, d), BF16),
        compiler_params=_params(("arbitrary",)),
        name="rmsnorm",
    )(x, g.reshape(1, d))


def _mm_body(a_ref, w_ref, o_ref):
    acc = lax.dot_general(a_ref[...], w_ref[...].astype(BF16), _NN, preferred_element_type=F32)
    o_ref[...] = acc.astype(o_ref.dtype)


def _matmul(a, w, *, col0, ncols, tm, tn, out_dtype, name):
    t, k = a.shape
    return pl.pallas_call(
        _mm_body,
        grid=(t // tm, ncols // tn),
        in_specs=[pl.BlockSpec((tm, k), lambda i, j: (i, 0), pipeline_mode=pl.Buffered(1)),
                  pl.BlockSpec((k, tn), lambda i, j: (0, j + col0 // tn))],
        out_specs=pl.BlockSpec((tm, tn), lambda i, j: (i, j)),
        out_shape=jax.ShapeDtypeStruct((t, ncols), out_dtype),
        compiler_params=_params(("arbitrary", "arbitrary")),
        name=name,
    )(a, w)


def _gates_body(a_ref, w_ref, o_ref, prev_ref, *, shift, tn):
    j = pl.program_id(1)
    cur = lax.dot_general(a_ref[...], w_ref[...].astype(BF16), _NN, preferred_element_type=F32)

    @pl.when(j > 0)
    def _():
        lane = lax.broadcasted_iota(I32, cur.shape, 1)
        joined = jnp.where(lane < shift, cur, prev_ref[...])
        o_ref[...] = jax.nn.sigmoid(pltpu.roll(joined, tn - shift, 1)).astype(o_ref.dtype)

    prev_ref[...] = cur


def _gates_proj(a, w, *, col0, ncols, tm, tn):
    t, k = a.shape
    shift = col0 % tn
    nb = ncols // tn
    return pl.pallas_call(
        functools.partial(_gates_body, shift=shift, tn=tn),
        grid=(t // tm, nb + 1),
        in_specs=[pl.BlockSpec((tm, k), lambda i, j: (i, 0), pipeline_mode=pl.Buffered(1)),
                  pl.BlockSpec((k, tn), lambda i, j: (0, j + col0 // tn))],
        out_specs=pl.BlockSpec((tm, tn), lambda i, j: (i, jnp.maximum(j - 1, 0))),
        out_shape=jax.ShapeDtypeStruct((t, ncols), BF16),
        scratch_shapes=[pltpu.VMEM((tm, tn), F32)],
        compiler_params=_params(("arbitrary", "arbitrary")),
        name="proj_gates",
    )(a, w)


def _qproj_body(a_ref, w_ref, cos_ref, sin_ref, o_ref, *, first_rope_block, tn):
    acc = lax.dot_general(a_ref[...], w_ref[...].astype(BF16), _NN, preferred_element_type=F32)
    j = pl.program_id(1)

    @pl.when(j < first_rope_block)
    def _():
        o_ref[...] = acc.astype(o_ref.dtype)

    @pl.when(j >= first_rope_block)
    def _():
        reps = tn // 128
        c = jnp.concatenate([cos_ref[...]] * reps, axis=1)
        s = jnp.concatenate([sin_ref[...]] * reps, axis=1)
        lane = lax.broadcasted_iota(I32, acc.shape, 1)
        first = (lane & (MLA_ROPE_DIM - 1)) < MLA_ROPE_DIM // 2
        half = MLA_ROPE_DIM // 2
        partner = jnp.where(first, pltpu.roll(acc, tn - half, 1), pltpu.roll(acc, half, 1))
        o_ref[...] = (acc * c + partner * s).astype(o_ref.dtype)


def _qproj(a, w, cos, sin, *, rope_col0, tm, tn):
    t, k = a.shape
    n = w.shape[1]
    return pl.pallas_call(
        functools.partial(_qproj_body, first_rope_block=rope_col0 // tn, tn=tn),
        grid=(t // tm, n // tn),
        in_specs=[pl.BlockSpec((tm, k), lambda i, j: (i, 0), pipeline_mode=pl.Buffered(1)),
                  pl.BlockSpec((k, tn), lambda i, j: (0, j)),
                  pl.BlockSpec((tm, 128), lambda i, j: (i, 0)),
                  pl.BlockSpec((tm, 128), lambda i, j: (i, 0))],
        out_specs=pl.BlockSpec((tm, tn), lambda i, j: (i, j)),
        out_shape=jax.ShapeDtypeStruct((t, n), BF16),
        compiler_params=_params(("arbitrary", "arbitrary")),
        name="q_proj_rope",
    )(a, w, cos, sin)


def _merge_body(oa_ref, ob_ref, wa_ref, wb_ref, ga_ref, gb_ref, o_ref):
    pa = lax.dot_general(oa_ref[...], wa_ref[...].astype(BF16), _NN, preferred_element_type=F32)
    pb = lax.dot_general(ob_ref[...], wb_ref[...].astype(BF16), _NN, preferred_element_type=F32)
    o_ref[...] = (ga_ref[...] * pa + gb_ref[...] * pb).astype(o_ref.dtype)


def _merge(oa, ob, wa, wb, gates, *, tm, tn):
    t, k = oa.shape
    n = wa.shape[1]
    nb = n // tn
    return pl.pallas_call(
        _merge_body,
        grid=(t // tm, nb),
        in_specs=[pl.BlockSpec((tm, k), lambda i, j: (i, 0), pipeline_mode=pl.Buffered(1)),
                  pl.BlockSpec((tm, k), lambda i, j: (i, 0), pipeline_mode=pl.Buffered(1)),
                  pl.BlockSpec((k, tn), lambda i, j: (0, j)),
                  pl.BlockSpec((k, tn), lambda i, j: (0, j)),
                  pl.BlockSpec((tm, tn), lambda i, j: (i, j)),
                  pl.BlockSpec((tm, tn), lambda i, j: (i, j + nb))],
        out_specs=pl.BlockSpec((tm, tn), lambda i, j: (i, j)),
        out_shape=jax.ShapeDtypeStruct((t, n), BF16),
        compiler_params=_params(("arbitrary", "arbitrary")),
        name="gated_merge",
    )(oa, ob, wa, wb, gates, gates)


def _outproj_body(a_ref, w_ref, r_ref, o_ref):
    acc = lax.dot_general(a_ref[...], w_ref[...].astype(BF16), _NN, preferred_element_type=F32)
    o_ref[...] = r_ref[...] + acc


def _outproj(a, w, resid, *, tm, tn):
    t, k = a.shape
    n = w.shape[1]
    return pl.pallas_call(
        _outproj_body,
        grid=(t // tm, n // tn),
        in_specs=[pl.BlockSpec((tm, k), lambda i, j: (i, 0), pipeline_mode=pl.Buffered(1)),
                  pl.BlockSpec((k, tn), lambda i, j: (0, j)),
                  pl.BlockSpec((tm, tn), lambda i, j: (i, j))],
        out_specs=pl.BlockSpec((tm, tn), lambda i, j: (i, j)),
        out_shape=jax.ShapeDtypeStruct((t, n), F32),
        compiler_params=_params(("arbitrary", "arbitrary")),
        name="out_proj_residual",
    )(a, w, resid)


def _mla_prep_body(lat_ref, pos_ref, qn_ref, kvn_ref, invf_ref,
                   cq_ref, ckv_ref, kpe_ref, cos_ref, sin_ref, *, q_rank, kv_rank):
    cq_ref[...] = _rms(lat_ref[:, 0:q_rank], qn_ref[...]).astype(cq_ref.dtype)
    ckv_ref[...] = _rms(lat_ref[:, q_rank:q_rank + kv_rank], kvn_ref[...]).astype(ckv_ref.dtype)
    kr = lat_ref[:, q_rank + kv_rank:q_rank + kv_rank + 128]
    ang = pos_ref[...].astype(F32) * invf_ref[...]
    c = jnp.cos(ang)
    s = jnp.sin(ang)
    lane = lax.broadcasted_iota(I32, ang.shape, 1)
    half = MLA_ROPE_DIM // 2
    first = (lane & (MLA_ROPE_DIM - 1)) < half
    s = jnp.where(first, -s, s)
    partner = jnp.where(first, pltpu.roll(kr, 128 - half, 1), pltpu.roll(kr, half, 1))
    kpe = jnp.where(lane < MLA_ROPE_DIM, kr * c + partner * s, 0.0)
    kpe_ref[:, 0:128] = kpe.astype(kpe_ref.dtype)
    kpe_ref[:, 128:256] = pltpu.roll(kpe, MLA_ROPE_DIM, 1).astype(kpe_ref.dtype)
    cos_ref[...] = c
    sin_ref[...] = s


def _mla_prep(lat, positions, q_norm, kv_norm, inv_freq128, *, tm=512):
    t = lat.shape[0]
    q_rank, kv_rank = q_norm.shape[0], kv_norm.shape[0]
    return pl.pallas_call(
        functools.partial(_mla_prep_body, q_rank=q_rank, kv_rank=kv_rank),
        grid=(t // tm,),
        in_specs=[pl.BlockSpec((tm, lat.shape[1]), lambda i: (i, 0)),
                  pl.BlockSpec((tm, 1), lambda i: (i, 0)),
                  pl.BlockSpec((1, q_rank), lambda i: (0, 0)),
                  pl.BlockSpec((1, kv_rank), lambda i: (0, 0)),
                  pl.BlockSpec((1, 128), lambda i: (0, 0))],
        out_specs=[pl.BlockSpec((tm, q_rank), lambda i: (i, 0)),
                   pl.BlockSpec((tm, kv_rank), lambda i: (i, 0)),
                   pl.BlockSpec((tm, 256), lambda i: (i, 0)),
                   pl.BlockSpec((tm, 128), lambda i: (i, 0)),
                   pl.BlockSpec((tm, 128), lambda i: (i, 0))],
        out_shape=[jax.ShapeDtypeStruct((t, q_rank), BF16),
                   jax.ShapeDtypeStruct((t, kv_rank), BF16),
                   jax.ShapeDtypeStruct((t, 256), BF16),
                   jax.ShapeDtypeStruct((t, 128), F32),
                   jax.ShapeDtypeStruct((t, 128), F32)],
        compiler_params=_params(("arbitrary",)),
        name="mla_prep",
    )(lat, positions.reshape(t, 1), q_norm.reshape(1, -1), kv_norm.reshape(1, -1), inv_freq128)


def _mla_body(qn_ref, qp_ref, kv_ref, kpe_ref, o_ref, kf_ref, vt_ref, *, scale):
    qi = pl.program_id(2)

    @pl.when(qi == 0)
    def _():
        for hh in range(2):
            kf_ref[hh, :, 0:128] = kv_ref[:, 256 * hh:256 * hh + 128]
            kf_ref[hh, :, 128:256] = kpe_ref[:, 128 * hh:128 * hh + 128]
            v = kv_ref[:, 256 * hh + 128:256 * hh + 256]
            vt_ref[hh] = v.astype(F32).T.astype(BF16)

    lane = lax.broadcasted_iota(I32, qp_ref.shape, 1)
    for hh in range(2):
        qp = qp_ref[...]
        keep = (lane >= MLA_ROPE_DIM) if hh else (lane < MLA_ROPE_DIM)
        qp = jnp.where(keep, qp, jnp.zeros_like(qp))
        qf = jnp.concatenate([qn_ref[:, 128 * hh:128 * hh + 128], qp], axis=1)
        st = lax.dot_general(kf_ref[hh], qf, _NT, preferred_element_type=F32)
        m = jnp.max(st, axis=0, keepdims=True)
        e = jnp.exp2((st - m) * (scale * LOG2_E))
        l = jnp.sum(e, axis=0, keepdims=True)
        ot = lax.dot_general(vt_ref[hh], e.astype(BF16), _NN, preferred_element_type=F32)
        o_ref[:, 128 * hh:128 * hh + 128] = (ot / l).T.astype(o_ref.dtype)


def _mla_attention(q2, kv, kpe, *, batch, seq, tq=512):
    t = q2.shape[0]
    nq = seq // tq
    hp = MLA_HEADS // 2
    nope_w = MLA_HEADS * MLA_NOPE_DIM
    scale = float((MLA_NOPE_DIM + MLA_ROPE_DIM) ** -0.5)
    return pl.pallas_call(
        functools.partial(_mla_body, scale=scale),
        grid=(batch, hp, nq),
        in_specs=[pl.BlockSpec((tq, 256), lambda b, h, q: (b * nq + q, h)),
                  pl.BlockSpec((tq, 128), lambda b, h, q: (b * nq + q, nope_w // 128 + h)),
                  pl.BlockSpec((seq, 512), lambda b, h, q: (b, h)),
                  pl.BlockSpec((seq, 256), lambda b, h, q: (b, 0))],
        out_specs=pl.BlockSpec((tq, 256), lambda b, h, q: (b * nq + q, h)),
        out_shape=jax.ShapeDtypeStruct((t, MLA_HEADS * MLA_V_DIM), BF16),
        scratch_shapes=[pltpu.VMEM((2, seq, 256), BF16), pltpu.VMEM((2, 128, seq), BF16)],
        compiler_params=_params(("arbitrary", "arbitrary", "arbitrary")),
        name="mla_attention",
    )(q2, q2, kv, kpe)


NA_Q_ROWS = 4
NA_K_ROWS = 12


def _na_group(g, rows):
    r0 = NA_Q_ROWS * g
    w0 = min(max(r0 - NA_WIN_ROWS // 2, 0), rows - NA_K_ROWS)
    if g == 0:
        var = 0
    elif g == rows // NA_Q_ROWS - 1:
        var = 2
    else:
        var = 1
    return r0, w0, var


def _na_bias_table(rpb, rows):
    n_heads = rpb.shape[0]
    kj = np.arange(GRID_W)[:, None]
    c = np.arange(GRID_W)[None, :]
    cs = np.clip(c - NA_WIN_COLS // 2, 0, GRID_W - NA_WIN_COLS)
    col_ok = (kj >= cs) & (kj < cs + NA_WIN_COLS)
    dc = kj - c + NA_WIN_COLS - 1
    pick = np.stack([(dc == dd) & col_ok for dd in range(2 * NA_WIN_COLS - 1)]).astype(np.float32)
    toep = jnp.einsum("hrd,dkc->hrkc", rpb.astype(F32), pick, precision=lax.Precision.HIGHEST)
    n_dr = 2 * NA_WIN_ROWS - 1
    row_sel = np.zeros((3, NA_K_ROWS, NA_Q_ROWS, n_dr), np.float32)
    for v, g in enumerate((0, 1, rows // NA_Q_ROWS - 1)):
        r0, w0, _ = _na_group(g, rows)
        for kl in range(NA_K_ROWS):
            for ql in range(NA_Q_ROWS):
                ki, r = w0 + kl, r0 + ql
                rs = min(max(r - NA_WIN_ROWS // 2, 0), rows - NA_WIN_ROWS)
                if rs <= ki < rs + NA_WIN_ROWS:
                    row_sel[v, kl, ql, ki - r + NA_WIN_ROWS - 1] = 1.0
    table = jnp.einsum("vkqa,hajc->hvkjqc", row_sel, toep, precision=lax.Precision.HIGHEST)
    inside = (row_sel.sum(-1) > 0)[:, :, None, :, None] & col_ok[None, None, :, None, :]
    table = jnp.where(inside[None], table, NEG_BIG)
    return table.reshape(n_heads, 3, NA_K_ROWS * GRID_W, NA_Q_ROWS * GRID_W)


def _na_body(q_ref, k_ref, v_ref, bias_ref, o_ref, *, rows, scale):
    vt = v_ref[...].astype(F32).T.astype(BF16)
    nq, nk = NA_Q_ROWS * GRID_W, NA_K_ROWS * GRID_W
    for g in range(rows // NA_Q_ROWS):
        r0, w0, var = _na_group(g, rows)
        kwin = k_ref[w0 * GRID_W:w0 * GRID_W + nk, :]
        qg = q_ref[r0 * GRID_W:r0 * GRID_W + nq, :]
        st = lax.dot_general(kwin, qg, _NT, preferred_element_type=F32) * scale + bias_ref[0, var]
        m = jnp.max(st, axis=0, keepdims=True)
        e = jnp.exp(st - m)
        l = jnp.sum(e, axis=0, keepdims=True)
        ot = lax.dot_general(vt[:, w0 * GRID_W:w0 * GRID_W + nk], e.astype(BF16), _NN,
                             preferred_element_type=F32)
        o_ref[r0 * GRID_W:r0 * GRID_W + nq, :] = (ot / l).T.astype(o_ref.dtype)


def _na_attention(qkv, bias, *, batch, seq):
    t = qkv.shape[0]
    rows = seq // GRID_W
    nk, nq = NA_K_ROWS * GRID_W, NA_Q_ROWS * GRID_W
    scale = float(NA_HEAD_DIM ** -0.5)
    return pl.pallas_call(
        functools.partial(_na_body, rows=rows, scale=scale),
        grid=(NA_HEADS, batch),
        in_specs=[pl.BlockSpec((seq, NA_HEAD_DIM), lambda h, b: (b, h)),
                  pl.BlockSpec((seq, NA_HEAD_DIM), lambda h, b: (b, NA_HEADS + h)),
                  pl.BlockSpec((seq, NA_HEAD_DIM), lambda h, b: (b, 2 * NA_HEADS + h)),
                  pl.BlockSpec((1, 3, nk, nq), lambda h, b: (h, 0, 0, 0))],
        out_specs=pl.BlockSpec((seq, NA_HEAD_DIM), lambda h, b: (b, h)),
        out_shape=jax.ShapeDtypeStruct((t, NA_HEADS * NA_HEAD_DIM), BF16),
        compiler_params=_params(("arbitrary", "arbitrary")),
        name="na_attention",
    )(qkv, qkv, qkv, bias)


def _router_body(x_ref, g_ref, wr_ref, br_ref, hn_ref, idx_ref, wts_ref, rank_ref, cnt_ref, carry_ref,
                 *, n_exp, tr):
    i = pl.program_id(0)

    @pl.when(i == 0)
    def _():
        carry_ref[...] = jnp.zeros_like(carry_ref)

    y = _rms(x_ref[...], g_ref[...])
    yb = y.astype(BF16)
    hn_ref[...] = _pack_bf16_pairs(y)

    logits = lax.dot_general(wr_ref[...].astype(BF16), yb, _NT, preferred_element_type=F32) + br_ref[...]
    eid = lax.broadcasted_iota(I32, (n_exp, tr), 0).astype(F32)
    work = logits
    vals, sels = [], []
    for k in range(TOP_K):
        m = jnp.max(work, axis=0, keepdims=True)
        first = jnp.min(jnp.where(work == m, eid, float(n_exp)), axis=0, keepdims=True)
        sel = eid == first
        vals.append(m)
        sels.append(sel)
        idx_ref[k:k + 1, :] = first.astype(I32)
        work = jnp.where(sel, -jnp.inf, work)
    es = [jnp.exp(v - vals[0]) for v in vals]
    denom = es[0] + es[1] + es[2] + es[3]
    for k in range(TOP_K):
        wts_ref[k:k + 1, :] = es[k] / denom

    chosen = jnp.zeros((n_exp, tr), F32)
    for sel in sels:
        chosen = chosen + sel.astype(F32)
    before = (lax.broadcasted_iota(I32, (tr, tr), 0) < lax.broadcasted_iota(I32, (tr, tr), 1)).astype(BF16)
    carry = carry_ref[:, 0:1]
    base = lax.dot_general(chosen.astype(BF16), before, _NN, preferred_element_type=F32) + carry
    for k in range(TOP_K):
        rank_ref[k:k + 1, :] = jnp.sum(jnp.where(sels[k], base, 0.0), axis=0, keepdims=True).astype(I32)
    total = carry + jnp.sum(chosen, axis=1, keepdims=True)
    carry_ref[...] = jnp.broadcast_to(total, carry_ref.shape)
    cnt_ref[...] = jnp.broadcast_to(total, cnt_ref.shape).astype(I32)


def _router(x1, g, wr_t, br, *, tr=512):
    t, d = x1.shape
    n_exp = wr_t.shape[0]
    return pl.pallas_call(
        functools.partial(_router_body, n_exp=n_exp, tr=tr),
        grid=(t // tr,),
        in_specs=[pl.BlockSpec((tr, d), lambda i: (i, 0)),
                  pl.BlockSpec((1, d), lambda i: (0, 0)),
                  pl.BlockSpec((n_exp, d), lambda i: (0, 0)),
                  pl.BlockSpec((n_exp, 1), lambda i: (0, 0))],
        out_specs=[pl.BlockSpec((tr, d // 2), lambda i: (i, 0)),
                   pl.BlockSpec((TOP_K, tr), lambda i: (0, i)),
                   pl.BlockSpec((TOP_K, tr), lambda i: (0, i)),
                   pl.BlockSpec((TOP_K, tr), lambda i: (0, i)),
                   pl.BlockSpec((n_exp, 128), lambda i: (0, 0))],
        out_shape=[jax.ShapeDtypeStruct((t, d // 2), U32),
                   jax.ShapeDtypeStruct((TOP_K, t), I32),
                   jax.ShapeDtypeStruct((TOP_K, t), F32),
                   jax.ShapeDtypeStruct((TOP_K, t), I32),
                   jax.ShapeDtypeStruct((n_exp, 128), I32)],
        scratch_shapes=[pltpu.VMEM((n_exp, 128), F32)],
        compiler_params=_params(("arbitrary",)),
        name="router_topk",
    )(x1, g.reshape(1, d), wr_t, br.reshape(n_exp, 1))


def _dispatch_body(pos_ref, hn_ref, xg_ref, sem, *, td):
    def row_copy(t, k):
        return pltpu.make_async_copy(hn_ref.at[pl.ds(t, 1), :],
                                     xg_ref.at[pl.ds(pos_ref[k, t], 1), :], sem)

    def start(t, c):
        for k in range(TOP_K):
            row_copy(t, k).start()
        return c

    def wait(t, c):
        for k in range(TOP_K):
            row_copy(t, k).wait()
        return c

    lax.fori_loop(0, td, start, 0)
    lax.fori_loop(0, td, wait, 0)


def _dispatch(pos, hn, n_rows, *, td=512):
    t = hn.shape[0]
    return pl.pallas_call(
        functools.partial(_dispatch_body, td=td),
        grid=(t // td,),
        in_specs=[pl.BlockSpec((TOP_K, td), lambda i: (0, i), memory_space=pltpu.SMEM),
                  pl.BlockSpec((td, hn.shape[1]), lambda i: (i, 0))],
        out_specs=pl.BlockSpec(memory_space=pl.ANY),
        out_shape=jax.ShapeDtypeStruct((n_rows, hn.shape[1]), hn.dtype),
        scratch_shapes=[pltpu.SemaphoreType.DMA(())],
        compiler_params=_params(("arbitrary",)),
        name="moe_dispatch",
    )(pos, hn)


def _expert_up_body(ie_ref, ib_ref, ins_ref, inr_ref, x_ref, w_ref, b_ref, o_ref, *, tn):
    w = pl.program_id(0)
    nsub = ins_ref[w]
    nrows = inr_ref[w]
    half = w_ref.shape[0] // 2

    for n in range(1, ITEM_SUBS + 1):
        @pl.when(nsub == n)
        def _(n=n):
            m = n * SUB
            xw = x_ref[0:m, :]
            row = lax.broadcasted_iota(I32, (m, 1), 0)
            xw = jnp.where(row < nrows, xw, jnp.uint32(0))
            lo, hi = _unpack_bf16_pairs(xw)
            gu = (lax.dot_general(lo.astype(BF16), w_ref[0:half, :].astype(BF16), _NN,
                                  preferred_element_type=F32)
                  + lax.dot_general(hi.astype(BF16), w_ref[half:, :].astype(BF16), _NN,
                                    preferred_element_type=F32)
                  + b_ref[...])
            gate = jnp.minimum(gu, SWIGLU_LIMIT)
            up = jnp.clip(pltpu.roll(gu, tn - 1, 1), -SWIGLU_LIMIT, SWIGLU_LIMIT)
            act = (up + 1.0) * gate * jax.nn.sigmoid(SWIGLU_ALPHA * gate)
            lane = lax.broadcasted_iota(I32, act.shape, 1)
            act = jnp.where((lane & 1) == 0, act, 0.0).astype(BF16)
            pick = (lax.broadcasted_iota(I32, (tn, tn // 2), 0)
                    == 2 * lax.broadcasted_iota(I32, (tn, tn // 2), 1)).astype(BF16)
            o_ref[0:m, :] = lax.dot_general(act, pick, _NN, preferred_element_type=F32).astype(o_ref.dtype)
            if m < ITEM_ROWS:
                o_ref[m:, :] = jnp.zeros((ITEM_ROWS - m, tn // 2), o_ref.dtype)


def _expert_up(items, xg, w_gu, b_gu, *, tn=512):
    item_e, item_blk, item_nsub, item_rows = items
    n_items = item_e.shape[0]
    n_exp, d, f2 = w_gu.shape
    nj = f2 // tn
    n_rows = xg.shape[0]

    def jmap(j, ns, w):
        return jnp.where(ns[w] > 0, j, nj - 1)

    grid_spec = pltpu.PrefetchScalarGridSpec(
        num_scalar_prefetch=4,
        grid=(n_items, nj),
        in_specs=[pl.BlockSpec((ITEM_ROWS, xg.shape[1]), lambda w, j, ie, ib, ns, nr: (ib[w], 0)),
                  pl.BlockSpec((None, d, tn), lambda w, j, ie, ib, ns, nr: (ie[w], 0, jmap(j, ns, w))),
                  pl.BlockSpec((None, 1, tn), lambda w, j, ie, ib, ns, nr: (ie[w], 0, jmap(j, ns, w)))],
        out_specs=pl.BlockSpec((ITEM_ROWS, tn // 2), lambda w, j, ie, ib, ns, nr: (ib[w], jmap(j, ns, w))),
    )
    return pl.pallas_call(
        functools.partial(_expert_up_body, tn=tn),
        grid_spec=grid_spec,
        out_shape=jax.ShapeDtypeStruct((n_rows, f2 // 2), BF16),
        compiler_params=_params(("arbitrary", "arbitrary")),
        name="expert_gate_up",
    )(item_e, item_blk, item_nsub, item_rows, xg, w_gu, b_gu.reshape(n_exp, 1, f2))


def _expert_down_body(ie_ref, ib_ref, ins_ref, a_ref, w_ref, b_ref, o_ref):
    w = pl.program_id(0)
    nsub = ins_ref[w]

    for n in range(1, ITEM_SUBS + 1):
        @pl.when(nsub == n)
        def _(n=n):
            m = n * SUB
            y = lax.dot_general(a_ref[0:m, :], w_ref[...].astype(BF16), _NN, preferred_element_type=F32)
            o_ref[0:m, :] = _pack_bf16_pairs(y + b_ref[...])
            if m < ITEM_ROWS:
                o_ref[m:, :] = jnp.zeros((ITEM_ROWS - m, o_ref.shape[1]), o_ref.dtype)


def _expert_down(items, act, w_d, b_d, *, tn=512):
    item_e, item_blk, item_nsub, _ = items
    n_items = item_e.shape[0]
    n_exp, f, d = w_d.shape
    nj = d // tn
    n_rows = act.shape[0]

    def jmap(j, ns, w):
        return jnp.where(ns[w] > 0, j, nj - 1)

    grid_spec = pltpu.PrefetchScalarGridSpec(
        num_scalar_prefetch=3,
        grid=(n_items, nj),
        in_specs=[pl.BlockSpec((ITEM_ROWS, f), lambda w, j, ie, ib, ns: (ib[w], 0)),
                  pl.BlockSpec((None, f, tn), lambda w, j, ie, ib, ns: (ie[w], 0, jmap(j, ns, w))),
                  pl.BlockSpec((None, 1, tn), lambda w, j, ie, ib, ns: (ie[w], 0, jmap(j, ns, w)))],
        out_specs=pl.BlockSpec((ITEM_ROWS, tn // 2), lambda w, j, ie, ib, ns: (ib[w], jmap(j, ns, w))),
    )
    return pl.pallas_call(
        _expert_down_body,
        grid_spec=grid_spec,
        out_shape=jax.ShapeDtypeStruct((n_rows, d // 2), U32),
        compiler_params=_params(("arbitrary", "arbitrary")),
        name="expert_down",
    )(item_e, item_blk, item_nsub, act, w_d, b_d.reshape(n_exp, 1, d))


def _combine_body(pos_ref, posn_ref, wts_ref, x_ref, g_ref, yg_ref, o_ref, buf_ref, sem, *, tc, pw):
    i = pl.program_id(0)
    n = pl.num_programs(0)
    slot = lax.rem(i, 2)

    def row_copy(p_ref, s, t, k):
        return pltpu.make_async_copy(yg_ref.at[pl.ds(p_ref[k, t], 1), :],
                                     buf_ref.at[s, k, pl.ds(t, 1), :], sem.at[s])

    def start_tile(p_ref, s):
        def body(t, c):
            for k in range(TOP_K):
                row_copy(p_ref, s, t, k).start()
            return c
        lax.fori_loop(0, tc, body, 0)

    @pl.when(i == 0)
    def _():
        start_tile(pos_ref, 0)

    @pl.when(i + 1 < n)
    def _():
        start_tile(posn_ref, 1 - slot)

    def wait_body(t, c):
        for k in range(TOP_K):
            row_copy(pos_ref, slot, t, k).wait()
        return c
    lax.fori_loop(0, tc, wait_body, 0)

    parts = []
    for j in range(buf_ref.shape[3] // pw):
        lo = hi = None
        for k in range(TOP_K):
            wk = wts_ref[:, k:k + 1]
            l, h = _unpack_bf16_pairs(buf_ref[slot, k, :, j * pw:(j + 1) * pw])
            lo = wk * l if lo is None else lo + wk * l
            hi = wk * h if hi is None else hi + wk * h
        parts += [lo, hi]
    o_ref[...] = _rms(x_ref[...] + jnp.concatenate(parts, axis=1), g_ref[...])


def _combine(pos, wts_t, x1, g, yg, *, pw, tc=128):
    t, d = x1.shape
    nt = t // tc
    return pl.pallas_call(
        functools.partial(_combine_body, tc=tc, pw=pw),
        grid=(nt,),
        in_specs=[pl.BlockSpec((TOP_K, tc), lambda i: (0, i), memory_space=pltpu.SMEM),
                  pl.BlockSpec((TOP_K, tc), lambda i: (0, jnp.minimum(i + 1, nt - 1)), memory_space=pltpu.SMEM),
                  pl.BlockSpec((tc, TOP_K), lambda i: (i, 0)),
                  pl.BlockSpec((tc, d), lambda i: (i, 0)),
                  pl.BlockSpec((1, d), lambda i: (0, 0)),
                  pl.BlockSpec(memory_space=pl.ANY)],
        out_specs=pl.BlockSpec((tc, d), lambda i: (i, 0)),
        out_shape=jax.ShapeDtypeStruct((t, d), F32),
        scratch_shapes=[pltpu.VMEM((2, TOP_K, tc, d // 2), U32), pltpu.SemaphoreType.DMA((2,))],
        compiler_params=_params(("arbitrary",)),
        name="moe_combine_norm",
    )(pos, pos, wts_t, x1, g.reshape(1, d), yg)


def _plan_items(counts, n_assign):
    n_exp = counts.shape[0]
    max_items = n_assign // ITEM_ROWS + n_exp
    nsub_e = (counts + SUB - 1) // SUB
    nitem_e = (counts + ITEM_ROWS - 1) // ITEM_ROWS
    last_item_e = jnp.cumsum(nitem_e)
    first_item_e = last_item_e - nitem_e
    total = last_item_e[-1]
    w = jnp.arange(max_items, dtype=I32)
    valid = w < total
    e_w = jnp.minimum(jnp.searchsorted(last_item_e, w, side="right"), n_exp - 1).astype(I32)
    e_last = e_w[jnp.maximum(total - 1, 0)]
    e_w = jnp.where(valid, e_w, e_last)
    c_w = w - first_item_e[e_w]
    nsub_w = jnp.where(valid, jnp.clip(nsub_e[e_w] - ITEM_SUBS * c_w, 0, ITEM_SUBS), 0).astype(I32)
    rows_w = jnp.where(valid, jnp.clip(counts[e_w] - ITEM_ROWS * c_w, 0, ITEM_ROWS), 0).astype(I32)
    blk_w = jnp.where(valid, w, max_items).astype(I32)
    row_off_e = (first_item_e * ITEM_ROWS).astype(I32)
    return (e_w, blk_w, nsub_w, rows_w), row_off_e, (max_items + 1) * ITEM_ROWS


def kernel(x, positions, norm_mix, w_in, q_a_norm, w_q_b, kv_a_norm, w_kv_b, na_rpb, w_proj_a, w_proj_b, w_out, norm_ffn, w_router, b_router, w_gate_up, b_gate_up, w_down, b_down, norm_final):
    b, s, d = x.shape
    t = b * s
    na_w = NA_HEADS * NA_HEAD_DIM
    q_rank, kv_rank = q_a_norm.shape[1], kv_a_norm.shape[1]
    lat0 = 3 * na_w
    gate0 = lat0 + q_rank + kv_rank + MLA_ROPE_DIM
    xf = x.reshape(t, d)
    w_in0 = w_in[0]

    hn = _rmsnorm(xf, norm_mix[0])
    qkv = _matmul(hn, w_in0, col0=0, ncols=lat0, tm=2048, tn=512, out_dtype=BF16, name="proj_qkv")
    lat = _matmul(hn, w_in0, col0=lat0, ncols=2048, tm=2048, tn=512, out_dtype=F32, name="proj_latent")
    gates = _gates_proj(hn, w_in0, col0=gate0, ncols=2 * d, tm=2048, tn=512)

    oa = _na_attention(qkv, _na_bias_table(na_rpb[0], s // GRID_W), batch=b, seq=s)

    half = MLA_ROPE_DIM // 2
    inv_freq = ROPE_THETA ** (-(jnp.arange(half, dtype=F32) * 2.0) / MLA_ROPE_DIM)
    inv_freq128 = jnp.tile(inv_freq, 128 // half).reshape(1, 128)
    cqn, ckvn, kpe, cos, sin = _mla_prep(lat, positions.reshape(t), q_a_norm[0], kv_a_norm[0], inv_freq128)
    qk_dim = MLA_NOPE_DIM + MLA_ROPE_DIM
    wq = w_q_b[0].reshape(q_rank, MLA_HEADS, qk_dim)
    wq = jnp.concatenate([wq[:, :, :MLA_NOPE_DIM].reshape(q_rank, -1),
                          wq[:, :, MLA_NOPE_DIM:].reshape(q_rank, -1)], axis=1)
    q2 = _qproj(cqn, wq, cos, sin, rope_col0=MLA_HEADS * MLA_NOPE_DIM, tm=2048, tn=512)
    kv = _matmul(ckvn, w_kv_b[0], col0=0, ncols=w_kv_b.shape[2], tm=2048, tn=1024, out_dtype=BF16,
                 name="kv_proj")
    ob = _mla_attention(q2, kv, kpe, batch=b, seq=s)

    y = _merge(oa, ob, w_proj_a[0], w_proj_b[0], gates, tm=2048, tn=512)
    x1 = _outproj(y, w_out[0], xf, tm=2048, tn=512)

    hn_packed, idx, wts, rank, counts = _router(x1, norm_ffn[0], w_router[0].T, b_router[0])
    items, row_off, n_rows = _plan_items(counts[:, 0], t * TOP_K)
    onehot = idx[None] == jnp.arange(row_off.shape[0], dtype=I32)[:, None, None]
    pos = jnp.sum(jnp.where(onehot, row_off[:, None, None], 0), axis=0) + rank
    xg = _dispatch(pos, hn_packed, n_rows)
    act = _expert_up(items, xg, w_gate_up[0], b_gate_up[0])
    down_tn = 512
    yg = _expert_down(items, act, w_down[0], b_down[0], tn=down_tn)
    out = _combine(pos, wts.T, x1, norm_final, yg, pw=down_tn // 2)
    return out.reshape(b, s, d)
```

```python
import functools

import numpy as np
import jax
import jax.numpy as jnp
from jax import lax
from jax.experimental import pallas as pl
from jax.experimental.pallas import tpu as pltpu

F32 = jnp.float32
BF16 = jnp.bfloat16
U32 = jnp.uint32
I32 = jnp.int32

GRID_W = 64
NA_HEADS = 16
NA_HEAD_DIM = 128
NA_WIN_ROWS = 8
NA_WIN_COLS = 16
MLA_HEADS = 16
MLA_NOPE_DIM = 128
MLA_ROPE_DIM = 64
MLA_V_DIM = 128
ROPE_THETA = 10000.0
TOP_K = 4
SWIGLU_LIMIT = 7.0
SWIGLU_ALPHA = 1.702
NORM_EPS = 1e-6

NEG_BIG = -1e30
LOG2_E = 1.4426950408889634

V7X_VMEM_BYTES = 64 * 1024 * 1024
VMEM_LIMIT = V7X_VMEM_BYTES - 4 * 1024 * 1024

SUB = 256
ITEM_SUBS = 5
ITEM_ROWS = SUB * ITEM_SUBS

_NN = (((1,), (0,)), ((), ()))
_NT = (((1,), (1,)), ((), ()))


def _params(sem=None):
    return pltpu.CompilerParams(vmem_limit_bytes=VMEM_LIMIT, dimension_semantics=sem)


def _rms(x, g):
    return x * lax.rsqrt(jnp.mean(x * x, axis=-1, keepdims=True) + NORM_EPS) * g


def _pack_bf16_pairs(x):
    n = x.shape[1] // 2
    lo = lax.bitcast_convert_type(x[:, :n].astype(BF16).astype(F32), U32)
    hi = lax.bitcast_convert_type(x[:, n:].astype(BF16).astype(F32), U32)
    return (hi & jnp.uint32(0xFFFF0000)) | (lo >> 16)


def _unpack_bf16_pairs(w):
    return (lax.bitcast_convert_type(w << 16, F32),
            lax.bitcast_convert_type(w & jnp.uint32(0xFFFF0000), F32))


def _rmsnorm_body(x_ref, g_ref, o_ref):
    o_ref[...] = _rms(x_ref[...], g_ref[...]).astype(o_ref.dtype)


def _rmsnorm(x, g, *, tm=512):
    t, d = x.shape
    return pl.pallas_call(
        _rmsnorm_body,
        grid=(t // tm,),
        in_specs=[pl.BlockSpec((tm, d), lambda i: (i, 0)),
                  pl.BlockSpec((1, d), lambda i: (0, 0))],
        out_specs=pl.BlockSpec((tm, d), lambda i: (i, 0)),
        out_shape=jax.ShapeDtypeStruct((t, d), BF16),
        compiler_params=_params(("arbitrary",)),
        name="rmsnorm",
    )(x, g.reshape(1, d))


def _mm_body(a_ref, w_ref, o_ref):
    acc = lax.dot_general(a_ref[...], w_ref[...].astype(BF16), _NN, preferred_element_type=F32)
    o_ref[...] = acc.astype(o_ref.dtype)


def _matmul(a, w, *, col0, ncols, tm, tn, out_dtype, name):
    t, k = a.shape
    return pl.pallas_call(
        _mm_body,
        grid=(t // tm, ncols // tn),
        in_specs=[pl.BlockSpec((tm, k), lambda i, j: (i, 0), pipeline_mode=pl.Buffered(1)),
                  pl.BlockSpec((k, tn), lambda i, j: (0, j + col0 // tn))],
        out_specs=pl.BlockSpec((tm, tn), lambda i, j: (i, j)),
        out_shape=jax.ShapeDtypeStruct((t, ncols), out_dtype),
        compiler_params=_params(("arbitrary", "arbitrary")),
        name=name,
    )(a, w)


def _mm_t_body(a_ref, wt_ref, o_ref, *, act):
    acc = lax.dot_general(a_ref[...], wt_ref[...].astype(BF16), _NT, preferred_element_type=F32)
    if act == "sigmoid":
        acc = jax.nn.sigmoid(acc)
    o_ref[...] = acc.astype(o_ref.dtype)


def _matmul_t(a, w_t, *, row0, ncols, tm, tn, out_dtype, act=None, name):
    t, k = a.shape
    return pl.pallas_call(
        functools.partial(_mm_t_body, act=act),
        grid=(t // tm, ncols // tn),
        in_specs=[pl.BlockSpec((tm, k), lambda i, j: (i, 0), pipeline_mode=pl.Buffered(1)),
                  pl.BlockSpec((pl.Element(tn), pl.Element(k)),
                               lambda i, j: (pl.multiple_of(row0 + j * tn, 8), 0))],
        out_specs=pl.BlockSpec((tm, tn), lambda i, j: (i, j)),
        out_shape=jax.ShapeDtypeStruct((t, ncols), out_dtype),
        compiler_params=_params(("arbitrary", "arbitrary")),
        name=name,
    )(a, w_t)


def _qproj_body(a_ref, w_ref, cos_ref, sin_ref, o_ref, *, first_rope_block, tn):
    acc = lax.dot_general(a_ref[...], w_ref[...].astype(BF16), _NN, preferred_element_type=F32)
    j = pl.program_id(1)

    @pl.when(j < first_rope_block)
    def _():
        o_ref[...] = acc.astype(o_ref.dtype)

    @pl.when(j >= first_rope_block)
    def _():
        reps = tn // 128
        c = jnp.concatenate([cos_ref[...]] * reps, axis=1)
        s = jnp.concatenate([sin_ref[...]] * reps, axis=1)
        lane = lax.broadcasted_iota(I32, acc.shape, 1)
        first = (lane & (MLA_ROPE_DIM - 1)) < MLA_ROPE_DIM // 2
        half = MLA_ROPE_DIM // 2
        partner = jnp.where(first, pltpu.roll(acc, tn - half, 1), pltpu.roll(acc, half, 1))
        o_ref[...] = (acc * c + partner * s).astype(o_ref.dtype)


def _qproj(a, w, cos, sin, *, rope_col0, tm, tn):
    t, k = a.shape
    n = w.shape[1]
    return pl.pallas_call(
        functools.partial(_qproj_body, first_rope_block=rope_col0 // tn, tn=tn),
        grid=(t // tm, n // tn),
        in_specs=[pl.BlockSpec((tm, k), lambda i, j: (i, 0), pipeline_mode=pl.Buffered(1)),
                  pl.BlockSpec((k, tn), lambda i, j: (0, j)),
                  pl.BlockSpec((tm, 128), lambda i, j: (i, 0)),
                  pl.BlockSpec((tm, 128), lambda i, j: (i, 0))],
        out_specs=pl.BlockSpec((tm, tn), lambda i, j: (i, j)),
        out_shape=jax.ShapeDtypeStruct((t, n), BF16),
        compiler_params=_params(("arbitrary", "arbitrary")),
        name="q_proj_rope",
    )(a, w, cos, sin)


def _merge_body(oa_ref, ob_ref, wa_ref, wb_ref, ga_ref, gb_ref, o_ref):
    pa = lax.dot_general(oa_ref[...], wa_ref[...].astype(BF16), _NN, preferred_element_type=F32)
    pb = lax.dot_general(ob_ref[...], wb_ref[...].astype(BF16), _NN, preferred_element_type=F32)
    o_ref[...] = (ga_ref[...] * pa + gb_ref[...] * pb).astype(o_ref.dtype)


def _merge(oa, ob, wa, wb, gates, *, tm, tn):
    t, k = oa.shape
    n = wa.shape[1]
    nb = n // tn
    return pl.pallas_call(
        _merge_body,
        grid=(t // tm, nb),
        in_specs=[pl.BlockSpec((tm, k), lambda i, j: (i, 0), pipeline_mode=pl.Buffered(1)),
                  pl.BlockSpec((tm, k), lambda i, j: (i, 0), pipeline_mode=pl.Buffered(1)),
                  pl.BlockSpec((k, tn), lambda i, j: (0, j)),
                  pl.BlockSpec((k, tn), lambda i, j: (0, j)),
                  pl.BlockSpec((tm, tn), lambda i, j: (i, j)),
                  pl.BlockSpec((tm, tn), lambda i, j: (i, j + nb))],
        out_specs=pl.BlockSpec((tm, tn), lambda i, j: (i, j)),
        out_shape=jax.ShapeDtypeStruct((t, n), BF16),
        compiler_params=_params(("arbitrary", "arbitrary")),
        name="gated_merge",
    )(oa, ob, wa, wb, gates, gates)


def _outproj_body(a_ref, w_ref, r_ref, o_ref):
    acc = lax.dot_general(a_ref[...], w_ref[...].astype(BF16), _NN, preferred_element_type=F32)
    o_ref[...] = r_ref[...] + acc


def _outproj(a, w, resid, *, tm, tn):
    t, k = a.shape
    n = w.shape[1]
    return pl.pallas_call(
        _outproj_body,
        grid=(t // tm, n // tn),
        in_specs=[pl.BlockSpec((tm, k), lambda i, j: (i, 0), pipeline_mode=pl.Buffered(1)),
                  pl.BlockSpec((k, tn), lambda i, j: (0, j)),
                  pl.BlockSpec((tm, tn), lambda i, j: (i, j))],
        out_specs=pl.BlockSpec((tm, tn), lambda i, j: (i, j)),
        out_shape=jax.ShapeDtypeStruct((t, n), F32),
        compiler_params=_params(("arbitrary", "arbitrary")),
        name="out_proj_residual",
    )(a, w, resid)


def _mla_prep_body(lat_ref, pos_ref, qn_ref, kvn_ref, invf_ref,
                   cq_ref, ckv_ref, kpe_ref, cos_ref, sin_ref, *, q_rank, kv_rank):
    cq_ref[...] = _rms(lat_ref[:, 0:q_rank], qn_ref[...]).astype(cq_ref.dtype)
    ckv_ref[...] = _rms(lat_ref[:, q_rank:q_rank + kv_rank], kvn_ref[...]).astype(ckv_ref.dtype)
    kr = lat_ref[:, q_rank + kv_rank:q_rank + kv_rank + 128]
    ang = pos_ref[...].astype(F32) * invf_ref[...]
    c = jnp.cos(ang)
    s = jnp.sin(ang)
    lane = lax.broadcasted_iota(I32, ang.shape, 1)
    half = MLA_ROPE_DIM // 2
    first = (lane & (MLA_ROPE_DIM - 1)) < half
    s = jnp.where(first, -s, s)
    partner = jnp.where(first, pltpu.roll(kr, 128 - half, 1), pltpu.roll(kr, half, 1))
    kpe = jnp.where(lane < MLA_ROPE_DIM, kr * c + partner * s, 0.0)
    kpe_ref[:, 0:128] = kpe.astype(kpe_ref.dtype)
    kpe_ref[:, 128:256] = pltpu.roll(kpe, MLA_ROPE_DIM, 1).astype(kpe_ref.dtype)
    cos_ref[...] = c
    sin_ref[...] = s


def _mla_prep(lat, positions, q_norm, kv_norm, inv_freq128, *, tm=512):
    t = lat.shape[0]
    q_rank, kv_rank = q_norm.shape[0], kv_norm.shape[0]
    return pl.pallas_call(
        functools.partial(_mla_prep_body, q_rank=q_rank, kv_rank=kv_rank),
        grid=(t // tm,),
        in_specs=[pl.BlockSpec((tm, lat.shape[1]), lambda i: (i, 0)),
                  pl.BlockSpec((tm, 1), lambda i: (i, 0)),
                  pl.BlockSpec((1, q_rank), lambda i: (0, 0)),
                  pl.BlockSpec((1, kv_rank), lambda i: (0, 0)),
                  pl.BlockSpec((1, 128), lambda i: (0, 0))],
        out_specs=[pl.BlockSpec((tm, q_rank), lambda i: (i, 0)),
                   pl.BlockSpec((tm, kv_rank), lambda i: (i, 0)),
                   pl.BlockSpec((tm, 256), lambda i: (i, 0)),
                   pl.BlockSpec((tm, 128), lambda i: (i, 0)),
                   pl.BlockSpec((tm, 128), lambda i: (i, 0))],
        out_shape=[jax.ShapeDtypeStruct((t, q_rank), BF16),
                   jax.ShapeDtypeStruct((t, kv_rank), BF16),
                   jax.ShapeDtypeStruct((t, 256), BF16),
                   jax.ShapeDtypeStruct((t, 128), F32),
                   jax.ShapeDtypeStruct((t, 128), F32)],
        compiler_params=_params(("arbitrary",)),
        name="mla_prep",
    )(lat, positions.reshape(t, 1), q_norm.reshape(1, -1), kv_norm.reshape(1, -1), inv_freq128)


def _mla_body(qn_ref, qp_ref, kv_ref, kpe_ref, o_ref, kf_ref, vt_ref, *, scale):
    qi = pl.program_id(2)

    @pl.when(qi == 0)
    def _():
        for hh in range(2):
            kf_ref[hh, :, 0:128] = kv_ref[:, 256 * hh:256 * hh + 128]
            kf_ref[hh, :, 128:256] = kpe_ref[:, 128 * hh:128 * hh + 128]
            v = kv_ref[:, 256 * hh + 128:256 * hh + 256]
            vt_ref[hh] = v.astype(F32).T.astype(BF16)

    lane = lax.broadcasted_iota(I32, qp_ref.shape, 1)
    for hh in range(2):
        qp = qp_ref[...]
        keep = (lane >= MLA_ROPE_DIM) if hh else (lane < MLA_ROPE_DIM)
        qp = jnp.where(keep, qp, jnp.zeros_like(qp))
        qf = jnp.concatenate([qn_ref[:, 128 * hh:128 * hh + 128], qp], axis=1)
        st = lax.dot_general(kf_ref[hh], qf, _NT, preferred_element_type=F32)
        m = jnp.max(st, axis=0, keepdims=True)
        e = jnp.exp2((st - m) * (scale * LOG2_E))
        l = jnp.sum(e, axis=0, keepdims=True)
        ot = lax.dot_general(vt_ref[hh], e.astype(BF16), _NN, preferred_element_type=F32)
        o_ref[:, 128 * hh:128 * hh + 128] = (ot / l).T.astype(o_ref.dtype)


def _mla_attention(q2, kv, kpe, *, batch, seq, tq=512):
    t = q2.shape[0]
    nq = seq // tq
    hp = MLA_HEADS // 2
    nope_w = MLA_HEADS * MLA_NOPE_DIM
    scale = float((MLA_NOPE_DIM + MLA_ROPE_DIM) ** -0.5)
    return pl.pallas_call(
        functools.partial(_mla_body, scale=scale),
        grid=(batch, hp, nq),
        in_specs=[pl.BlockSpec((tq, 256), lambda b, h, q: (b * nq + q, h)),
                  pl.BlockSpec((tq, 128), lambda b, h, q: (b * nq + q, nope_w // 128 + h)),
                  pl.BlockSpec((seq, 512), lambda b, h, q: (b, h)),
                  pl.BlockSpec((seq, 256), lambda b, h, q: (b, 0))],
        out_specs=pl.BlockSpec((tq, 256), lambda b, h, q: (b * nq + q, h)),
        out_shape=jax.ShapeDtypeStruct((t, MLA_HEADS * MLA_V_DIM), BF16),
        scratch_shapes=[pltpu.VMEM((2, seq, 256), BF16), pltpu.VMEM((2, 128, seq), BF16)],
        compiler_params=_params(("arbitrary", "arbitrary", "arbitrary")),
        name="mla_attention",
    )(q2, q2, kv, kpe)


NA_Q_ROWS = 4
NA_K_ROWS = 12
NA_PAD_BLOCKS = NA_Q_ROWS
NA_BIAS_BLOCKS = 2 * NA_WIN_ROWS - 1 + 2 * NA_PAD_BLOCKS + 1


def _na_group(g, rows):
    r0 = NA_Q_ROWS * g
    w0 = min(max(r0 - NA_WIN_ROWS // 2, 0), rows - NA_K_ROWS)
    return r0, w0


def _na_bias_tables(rpb):
    n_heads = rpb.shape[0]
    kj = np.arange(GRID_W)[:, None]
    c = np.arange(GRID_W)[None, :]
    cs = np.clip(c - NA_WIN_COLS // 2, 0, GRID_W - NA_WIN_COLS)
    col_ok = (kj >= cs) & (kj < cs + NA_WIN_COLS)
    dc = kj - c + NA_WIN_COLS - 1
    pick = np.stack([(dc == dd) & col_ok for dd in range(2 * NA_WIN_COLS - 1)]).astype(np.float32)
    toep = jnp.einsum("hrd,dkc->hrkc", rpb.astype(F32), pick, precision=lax.Precision.HIGHEST)
    toep = jnp.where(col_ok[None, None], toep, NEG_BIG)
    n_off = 2 * NA_WIN_ROWS - 1
    blocks = jnp.pad(toep[:, ::-1], ((0, 0), (NA_PAD_BLOCKS, NA_BIAS_BLOCKS - NA_PAD_BLOCKS - n_off), (0, 0), (0, 0)),
                     constant_values=NEG_BIG)
    even = blocks.transpose(0, 2, 1, 3).reshape(n_heads, GRID_W, NA_BIAS_BLOCKS * GRID_W)
    odd = jnp.pad(even[:, :, GRID_W:], ((0, 0), (0, 0), (0, GRID_W)), constant_values=NEG_BIG)
    return even, odd


def _na_group_bias(even_ref, odd_ref, g, rows):
    r0, w0 = _na_group(g, rows)
    nq = NA_Q_ROWS * GRID_W
    slabs = []
    for k in range(NA_K_ROWS):
        ki = w0 + k
        bad = []
        for q in range(NA_Q_ROWS):
            rs = min(max(r0 + q - NA_WIN_ROWS // 2, 0), rows - NA_WIN_ROWS)
            if not rs <= ki < rs + NA_WIN_ROWS:
                bad.append(q)
        if len(bad) == NA_Q_ROWS:
            slabs.append(jnp.full((GRID_W, nq), NEG_BIG, F32))
            continue
        i0 = NA_PAD_BLOCKS + (NA_WIN_ROWS - 1) - (ki - r0)
        ref, first = (even_ref, i0) if i0 % 2 == 0 else (odd_ref, i0 - 1)
        slab = ref[0, :, first * GRID_W:first * GRID_W + nq]
        if bad:
            q_of_lane = lax.broadcasted_iota(I32, slab.shape, 1) // GRID_W
            outside = q_of_lane == bad[0]
            for q in bad[1:]:
                outside = outside | (q_of_lane == q)
            slab = jnp.where(outside, NEG_BIG, slab)
        slabs.append(slab)
    return jnp.concatenate(slabs, axis=0)


def _na_body(q_ref, k_ref, v_ref, even_ref, odd_ref, o_ref, *, rows, scale):
    vt = v_ref[...].astype(F32).T.astype(BF16)
    nq, nk = NA_Q_ROWS * GRID_W, NA_K_ROWS * GRID_W
    for g in range(rows // NA_Q_ROWS):
        r0, w0 = _na_group(g, rows)
        kwin = k_ref[w0 * GRID_W:w0 * GRID_W + nk, :]
        qg = q_ref[r0 * GRID_W:r0 * GRID_W + nq, :]
        st = (lax.dot_general(kwin, qg, _NT, preferred_element_type=F32) * scale
              + _na_group_bias(even_ref, odd_ref, g, rows))
        m = jnp.max(st, axis=0, keepdims=True)
        e = jnp.exp(st - m)
        l = jnp.sum(e, axis=0, keepdims=True)
        ot = lax.dot_general(vt[:, w0 * GRID_W:w0 * GRID_W + nk], e.astype(BF16), _NN,
                             preferred_element_type=F32)
        o_ref[r0 * GRID_W:r0 * GRID_W + nq, :] = (ot / l).T.astype(o_ref.dtype)


def _na_attention(qkv, bias_even, bias_odd, *, batch, seq):
    t = qkv.shape[0]
    rows = seq // GRID_W
    scale = float(NA_HEAD_DIM ** -0.5)
    bias_spec = pl.BlockSpec((1, GRID_W, NA_BIAS_BLOCKS * GRID_W), lambda h, b: (h, 0, 0))
    return pl.pallas_call(
        functools.partial(_na_body, rows=rows, scale=scale),
        grid=(NA_HEADS, batch),
        in_specs=[pl.BlockSpec((seq, NA_HEAD_DIM), lambda h, b: (b, h)),
                  pl.BlockSpec((seq, NA_HEAD_DIM), lambda h, b: (b, NA_HEADS + h)),
                  pl.BlockSpec((seq, NA_HEAD_DIM), lambda h, b: (b, 2 * NA_HEADS + h)),
                  bias_spec, bias_spec],
        out_specs=pl.BlockSpec((seq, NA_HEAD_DIM), lambda h, b: (b, h)),
        out_shape=jax.ShapeDtypeStruct((t, NA_HEADS * NA_HEAD_DIM), BF16),
        compiler_params=_params(("arbitrary", "arbitrary")),
        name="na_attention",
    )(qkv, qkv, qkv, bias_even, bias_odd)


def _router_body(x_ref, g_ref, wr_ref, br_ref, hn_ref, idx_ref, wts_ref, rank_ref, cnt_ref, carry_ref,
                 *, n_exp, tr):
    i = pl.program_id(0)

    @pl.when(i == 0)
    def _():
        carry_ref[...] = jnp.zeros_like(carry_ref)

    y = _rms(x_ref[...], g_ref[...])
    yb = y.astype(BF16)
    hn_ref[...] = _pack_bf16_pairs(y)

    logits = lax.dot_general(wr_ref[...].astype(BF16), yb, _NT, preferred_element_type=F32) + br_ref[...]
    eid = lax.broadcasted_iota(I32, (n_exp, tr), 0).astype(F32)
    work = logits
    vals, sels = [], []
    for k in range(TOP_K):
        m = jnp.max(work, axis=0, keepdims=True)
        first = jnp.min(jnp.where(work == m, eid, float(n_exp)), axis=0, keepdims=True)
        sel = eid == first
        vals.append(m)
        sels.append(sel)
        idx_ref[k:k + 1, :] = first.astype(I32)
        work = jnp.where(sel, -jnp.inf, work)
    es = [jnp.exp(v - vals[0]) for v in vals]
    denom = es[0] + es[1] + es[2] + es[3]
    for k in range(TOP_K):
        wts_ref[k:k + 1, :] = es[k] / denom

    chosen = jnp.zeros((n_exp, tr), F32)
    for sel in sels:
        chosen = chosen + sel.astype(F32)
    before = (lax.broadcasted_iota(I32, (tr, tr), 0) < lax.broadcasted_iota(I32, (tr, tr), 1)).astype(BF16)
    carry = carry_ref[:, 0:1]
    base = lax.dot_general(chosen.astype(BF16), before, _NN, preferred_element_type=F32) + carry
    for k in range(TOP_K):
        rank_ref[k:k + 1, :] = jnp.sum(jnp.where(sels[k], base, 0.0), axis=0, keepdims=True).astype(I32)
    total = carry + jnp.sum(chosen, axis=1, keepdims=True)
    carry_ref[...] = jnp.broadcast_to(total, carry_ref.shape)
    cnt_ref[...] = jnp.broadcast_to(total, cnt_ref.shape).astype(I32)


def _router(x1, g, wr_t, br, *, tr=512):
    t, d = x1.shape
    n_exp = wr_t.shape[0]
    return pl.pallas_call(
        functools.partial(_router_body, n_exp=n_exp, tr=tr),
        grid=(t // tr,),
        in_specs=[pl.BlockSpec((tr, d), lambda i: (i, 0)),
                  pl.BlockSpec((1, d), lambda i: (0, 0)),
                  pl.BlockSpec((n_exp, d), lambda i: (0, 0)),
                  pl.BlockSpec((n_exp, 1), lambda i: (0, 0))],
        out_specs=[pl.BlockSpec((tr, d // 2), lambda i: (i, 0)),
                   pl.BlockSpec((TOP_K, tr), lambda i: (0, i)),
                   pl.BlockSpec((TOP_K, tr), lambda i: (0, i)),
                   pl.BlockSpec((TOP_K, tr), lambda i: (0, i)),
                   pl.BlockSpec((n_exp, 128), lambda i: (0, 0))],
        out_shape=[jax.ShapeDtypeStruct((t, d // 2), U32),
                   jax.ShapeDtypeStruct((TOP_K, t), I32),
                   jax.ShapeDtypeStruct((TOP_K, t), F32),
                   jax.ShapeDtypeStruct((TOP_K, t), I32),
                   jax.ShapeDtypeStruct((n_exp, 128), I32)],
        scratch_shapes=[pltpu.VMEM((n_exp, 128), F32)],
        compiler_params=_params(("arbitrary",)),
        name="router_topk",
    )(x1, g.reshape(1, d), wr_t, br.reshape(n_exp, 1))


def _dispatch_body(pos_ref, hn_ref, xg_ref, sem, *, td):
    def row_copy(t, k):
        return pltpu.make_async_copy(hn_ref.at[pl.ds(t, 1), :],
                                     xg_ref.at[pl.ds(pos_ref[k, t], 1), :], sem)

    def start(t, c):
        for k in range(TOP_K):
            row_copy(t, k).start()
        return c

    def wait(t, c):
        for k in range(TOP_K):
            row_copy(t, k).wait()
        return c

    lax.fori_loop(0, td, start, 0)
    lax.fori_loop(0, td, wait, 0)


def _dispatch(pos, hn, n_rows, *, td=512):
    t = hn.shape[0]
    return pl.pallas_call(
        functools.partial(_dispatch_body, td=td),
        grid=(t // td,),
        in_specs=[pl.BlockSpec((TOP_K, td), lambda i: (0, i), memory_space=pltpu.SMEM),
                  pl.BlockSpec((td, hn.shape[1]), lambda i: (i, 0))],
        out_specs=pl.BlockSpec(memory_space=pl.ANY),
        out_shape=jax.ShapeDtypeStruct((n_rows, hn.shape[1]), hn.dtype),
        scratch_shapes=[pltpu.SemaphoreType.DMA(())],
        compiler_params=_params(("arbitrary",)),
        name="moe_dispatch",
    )(pos, hn)


def _expert_up_body(ie_ref, ib_ref, ins_ref, inr_ref, x_ref, w_ref, b_ref, o_ref, *, tn):
    w = pl.program_id(0)
    nsub = ins_ref[w]
    nrows = inr_ref[w]
    half = w_ref.shape[0] // 2

    for n in range(1, ITEM_SUBS + 1):
        @pl.when(nsub == n)
        def _(n=n):
            m = n * SUB
            row = lax.broadcasted_iota(I32, (m, 1), 0)
            xw = jnp.where(row < nrows, x_ref[0:m, :], jnp.uint32(0))
            lo, hi = _unpack_bf16_pairs(xw)
            gu = (lax.dot_general(lo.astype(BF16), w_ref[0:half, :].astype(BF16), _NN,
                                  preferred_element_type=F32)
                  + lax.dot_general(hi.astype(BF16), w_ref[half:, :].astype(BF16), _NN,
                                    preferred_element_type=F32)
                  + b_ref[...])
            gate = jnp.minimum(gu, SWIGLU_LIMIT)
            up = jnp.clip(pltpu.roll(gu, tn - 1, 1), -SWIGLU_LIMIT, SWIGLU_LIMIT)
            act = (up + 1.0) * gate * jax.nn.sigmoid(SWIGLU_ALPHA * gate)
            lane = lax.broadcasted_iota(I32, act.shape, 1)
            act = jnp.where((lane & 1) == 0, act, 0.0).astype(BF16)
            pick = (lax.broadcasted_iota(I32, (tn, tn // 2), 0)
                    == 2 * lax.broadcasted_iota(I32, (tn, tn // 2), 1)).astype(BF16)
            o_ref[0:m, :] = lax.dot_general(act, pick, _NN, preferred_element_type=F32).astype(o_ref.dtype)
            if m < ITEM_ROWS:
                o_ref[m:, :] = jnp.zeros((ITEM_ROWS - m, tn // 2), o_ref.dtype)


def _expert_up(items, xg, w_gu, b_gu, *, tn=512):
    item_e, item_blk, item_nsub, item_rows = items
    n_items = item_e.shape[0]
    n_exp, d, f2 = w_gu.shape
    nj = f2 // tn
    n_rows = xg.shape[0]

    def jmap(j, ns, w):
        return jnp.where(ns[w] > 0, j, nj - 1)

    grid_spec = pltpu.PrefetchScalarGridSpec(
        num_scalar_prefetch=4,
        grid=(n_items, nj),
        in_specs=[pl.BlockSpec((ITEM_ROWS, xg.shape[1]), lambda w, j, ie, ib, ns, nr: (ib[w], 0)),
                  pl.BlockSpec((None, d, tn), lambda w, j, ie, ib, ns, nr: (ie[w], 0, jmap(j, ns, w))),
                  pl.BlockSpec((None, 1, tn), lambda w, j, ie, ib, ns, nr: (ie[w], 0, jmap(j, ns, w)))],
        out_specs=pl.BlockSpec((ITEM_ROWS, tn // 2), lambda w, j, ie, ib, ns, nr: (ib[w], jmap(j, ns, w))),
    )
    return pl.pallas_call(
        functools.partial(_expert_up_body, tn=tn),
        grid_spec=grid_spec,
        out_shape=jax.ShapeDtypeStruct((n_rows, f2 // 2), BF16),
        compiler_params=_params(("arbitrary", "arbitrary")),
        name="expert_gate_up",
    )(item_e, item_blk, item_nsub, item_rows, xg, w_gu, b_gu.reshape(n_exp, 1, f2))


def _expert_down_body(ie_ref, ib_ref, ins_ref, a_ref, w_ref, b_ref, o_ref):
    w = pl.program_id(0)
    nsub = ins_ref[w]

    for n in range(1, ITEM_SUBS + 1):
        @pl.when(nsub == n)
        def _(n=n):
            m = n * SUB
            y = lax.dot_general(a_ref[0:m, :], w_ref[...].astype(BF16), _NN, preferred_element_type=F32)
            o_ref[0:m, :] = _pack_bf16_pairs(y + b_ref[...])
            if m < ITEM_ROWS:
                o_ref[m:, :] = jnp.zeros((ITEM_ROWS - m, o_ref.shape[1]), o_ref.dtype)


def _expert_down(items, act, w_d, b_d, *, tn=512):
    item_e, item_blk, item_nsub, _ = items
    n_items = item_e.shape[0]
    n_exp, f, d = w_d.shape
    nj = d // tn
    n_rows = act.shape[0]

    def jmap(j, ns, w):
        return jnp.where(ns[w] > 0, j, nj - 1)

    grid_spec = pltpu.PrefetchScalarGridSpec(
        num_scalar_prefetch=3,
        grid=(n_items, nj),
        in_specs=[pl.BlockSpec((ITEM_ROWS, f), lambda w, j, ie, ib, ns: (ib[w], 0)),
                  pl.BlockSpec((None, f, tn), lambda w, j, ie, ib, ns: (ie[w], 0, jmap(j, ns, w))),
                  pl.BlockSpec((None, 1, tn), lambda w, j, ie, ib, ns: (ie[w], 0, jmap(j, ns, w)))],
        out_specs=pl.BlockSpec((ITEM_ROWS, tn // 2), lambda w, j, ie, ib, ns: (ib[w], jmap(j, ns, w))),
    )
    return pl.pallas_call(
        _expert_down_body,
        grid_spec=grid_spec,
        out_shape=jax.ShapeDtypeStruct((n_rows, d // 2), U32),
        compiler_params=_params(("arbitrary", "arbitrary")),
        name="expert_down",
    )(item_e, item_blk, item_nsub, act, w_d, b_d.reshape(n_exp, 1, d))


def _combine_body(pos_ref, posn_ref, wts_ref, x_ref, g_ref, yg_ref, o_ref, buf_ref, sem, *, tc, pw):
    i = pl.program_id(0)
    n = pl.num_programs(0)
    slot = lax.rem(i, 2)

    def row_copy(p_ref, s, t, k):
        return pltpu.make_async_copy(yg_ref.at[pl.ds(p_ref[k, t], 1), :],
                                     buf_ref.at[s, k, pl.ds(t, 1), :], sem.at[s])

    def start_tile(p_ref, s):
        def body(t, c):
            for k in range(TOP_K):
                row_copy(p_ref, s, t, k).start()
            return c
        lax.fori_loop(0, tc, body, 0)

    @pl.when(i == 0)
    def _():
        start_tile(pos_ref, 0)

    @pl.when(i + 1 < n)
    def _():
        start_tile(posn_ref, 1 - slot)

    def wait_body(t, c):
        for k in range(TOP_K):
            row_copy(pos_ref, slot, t, k).wait()
        return c
    lax.fori_loop(0, tc, wait_body, 0)

    parts = []
    for j in range(buf_ref.shape[3] // pw):
        lo = hi = None
        for k in range(TOP_K):
            wk = wts_ref[:, k:k + 1]
            l, h = _unpack_bf16_pairs(buf_ref[slot, k, :, j * pw:(j + 1) * pw])
            lo = wk * l if lo is None else lo + wk * l
            hi = wk * h if hi is None else hi + wk * h
        parts += [lo, hi]
    o_ref[...] = _rms(x_ref[...] + jnp.concatenate(parts, axis=1), g_ref[...])


def _combine(pos, wts_t, x1, g, yg, *, pw, tc=128):
    t, d = x1.shape
    nt = t // tc
    return pl.pallas_call(
        functools.partial(_combine_body, tc=tc, pw=pw),
        grid=(nt,),
        in_specs=[pl.BlockSpec((TOP_K, tc), lambda i: (0, i), memory_space=pltpu.SMEM),
                  pl.BlockSpec((TOP_K, tc), lambda i: (0, jnp.minimum(i + 1, nt - 1)), memory_space=pltpu.SMEM),
                  pl.BlockSpec((tc, TOP_K), lambda i: (i, 0)),
                  pl.BlockSpec((tc, d), lambda i: (i, 0)),
                  pl.BlockSpec((1, d), lambda i: (0, 0)),
                  pl.BlockSpec(memory_space=pl.ANY)],
        out_specs=pl.BlockSpec((tc, d), lambda i: (i, 0)),
        out_shape=jax.ShapeDtypeStruct((t, d), F32),
        scratch_shapes=[pltpu.VMEM((2, TOP_K, tc, d // 2), U32), pltpu.SemaphoreType.DMA((2,))],
        compiler_params=_params(("arbitrary",)),
        name="moe_combine_norm",
    )(pos, pos, wts_t, x1, g.reshape(1, d), yg)


def _plan_items(counts, n_assign):
    n_exp = counts.shape[0]
    max_items = n_assign // ITEM_ROWS + n_exp
    nsub_e = (counts + SUB - 1) // SUB
    nitem_e = (counts + ITEM_ROWS - 1) // ITEM_ROWS
    last_item_e = jnp.cumsum(nitem_e)
    first_item_e = last_item_e - nitem_e
    total = last_item_e[-1]
    w = jnp.arange(max_items, dtype=I32)
    valid = w < total
    e_w = jnp.minimum(jnp.searchsorted(last_item_e, w, side="right"), n_exp - 1).astype(I32)
    e_last = e_w[jnp.maximum(total - 1, 0)]
    e_w = jnp.where(valid, e_w, e_last)
    c_w = w - first_item_e[e_w]
    nsub_w = jnp.where(valid, jnp.clip(nsub_e[e_w] - ITEM_SUBS * c_w, 0, ITEM_SUBS), 0).astype(I32)
    rows_w = jnp.where(valid, jnp.clip(counts[e_w] - ITEM_ROWS * c_w, 0, ITEM_ROWS), 0).astype(I32)
    blk_w = jnp.where(valid, w, max_items).astype(I32)
    row_off_e = (first_item_e * ITEM_ROWS).astype(I32)
    return (e_w, blk_w, nsub_w, rows_w), row_off_e, (max_items + 1) * ITEM_ROWS


def kernel(x, positions, norm_mix, w_in, q_a_norm, w_q_b, kv_a_norm, w_kv_b, na_rpb, w_proj_a, w_proj_b, w_out, norm_ffn, w_router, b_router, w_gate_up, b_gate_up, w_down, b_down, norm_final):
    b, s, d = x.shape
    t = b * s
    na_w = NA_HEADS * NA_HEAD_DIM
    q_rank, kv_rank = q_a_norm.shape[1], kv_a_norm.shape[1]
    lat0 = 3 * na_w
    gate0 = lat0 + q_rank + kv_rank + MLA_ROPE_DIM
    xf = x.reshape(t, d)
    w_in_t = jnp.swapaxes(w_in, 1, 2)[0]

    hn = _rmsnorm(xf, norm_mix[0])
    qkv = _matmul_t(hn, w_in_t, row0=0, ncols=lat0, tm=2048, tn=512, out_dtype=BF16, name="proj_qkv")
    lat = _matmul_t(hn, w_in_t, row0=lat0, ncols=2048, tm=2048, tn=512, out_dtype=F32, name="proj_latent")
    gates = _matmul_t(hn, w_in_t, row0=gate0, ncols=2 * d, tm=2048, tn=512, out_dtype=BF16,
                      act="sigmoid", name="proj_gates")

    oa = _na_attention(qkv, *_na_bias_tables(na_rpb[0]), batch=b, seq=s)

    half = MLA_ROPE_DIM // 2
    inv_freq = ROPE_THETA ** (-(jnp.arange(half, dtype=F32) * 2.0) / MLA_ROPE_DIM)
    inv_freq128 = jnp.tile(inv_freq, 128 // half).reshape(1, 128)
    cqn, ckvn, kpe, cos, sin = _mla_prep(lat, positions.reshape(t), q_a_norm[0], kv_a_norm[0], inv_freq128)
    qk_dim = MLA_NOPE_DIM + MLA_ROPE_DIM
    wq = w_q_b[0].reshape(q_rank, MLA_HEADS, qk_dim)
    wq = jnp.concatenate([wq[:, :, :MLA_NOPE_DIM].reshape(q_rank, -1),
                          wq[:, :, MLA_NOPE_DIM:].reshape(q_rank, -1)], axis=1)
    q2 = _qproj(cqn, wq, cos, sin, rope_col0=MLA_HEADS * MLA_NOPE_DIM, tm=2048, tn=512)
    kv = _matmul(ckvn, w_kv_b[0], col0=0, ncols=w_kv_b.shape[2], tm=2048, tn=1024, out_dtype=BF16,
                 name="kv_proj")
    ob = _mla_attention(q2, kv, kpe, batch=b, seq=s)

    y = _merge(oa, ob, w_proj_a[0], w_proj_b[0], gates, tm=2048, tn=512)
    x1 = _outproj(y, w_out[0], xf, tm=2048, tn=512)

    hn_packed, idx, wts, rank, counts = _router(x1, norm_ffn[0], w_router[0].T, b_router[0])
    items, row_off, n_rows = _plan_items(counts[:, 0], t * TOP_K)
    onehot = idx[None] == jnp.arange(row_off.shape[0], dtype=I32)[:, None, None]
    pos = jnp.sum(jnp.where(onehot, row_off[:, None, None], 0), axis=0) + rank
    xg = _dispatch(pos, hn_packed, n_rows)
    act = _expert_up(items, xg, w_gate_up[0], b_gate_up[0])
    down_tn = 1024
    yg = _expert_down(items, act, w_down[0], b_down[0], tn=down_tn)
    out = _combine(pos, wts.T, x1, norm_final, yg, pw=down_tn // 2)
    return out.reshape(b, s, d)
```

```python
import functools

import numpy as np
import jax
import jax.numpy as jnp
from jax import lax
from jax.experimental import pallas as pl
from jax.experimental.pallas import tpu as pltpu

F32 = jnp.float32
BF16 = jnp.bfloat16
U32 = jnp.uint32
I32 = jnp.int32

GRID_W = 64
NA_HEADS = 16
NA_HEAD_DIM = 128
NA_WIN_ROWS = 8
NA_WIN_COLS = 16
MLA_HEADS = 16
MLA_NOPE_DIM = 128
MLA_ROPE_DIM = 64
MLA_V_DIM = 128
ROPE_THETA = 10000.0
TOP_K = 4
SWIGLU_LIMIT = 7.0
SWIGLU_ALPHA = 1.702
NORM_EPS = 1e-6

NEG_BIG = -1e30
LOG2_E = 1.4426950408889634

V7X_VMEM_BYTES = 64 * 1024 * 1024
VMEM_LIMIT = V7X_VMEM_BYTES - 4 * 1024 * 1024

SUB = 128
ITEM_SUBS = 10
ITEM_ROWS = SUB * ITEM_SUBS

_NN = (((1,), (0,)), ((), ()))
_NT = (((1,), (1,)), ((), ()))


def _params(sem=None):
    return pltpu.CompilerParams(vmem_limit_bytes=VMEM_LIMIT, dimension_semantics=sem)


def _rms(x, g):
    return x * lax.rsqrt(jnp.mean(x * x, axis=-1, keepdims=True) + NORM_EPS) * g


def _pack_bf16_pairs(x):
    n = x.shape[1] // 2
    lo = lax.bitcast_convert_type(x[:, :n].astype(BF16).astype(F32), U32)
    hi = lax.bitcast_convert_type(x[:, n:].astype(BF16).astype(F32), U32)
    return (hi & jnp.uint32(0xFFFF0000)) | (lo >> 16)


def _unpack_bf16_pairs(w):
    return (lax.bitcast_convert_type(w << 16, F32),
            lax.bitcast_convert_type(w & jnp.uint32(0xFFFF0000), F32))


def _rmsnorm_body(x_ref, g_ref, o_ref):
    o_ref[...] = _rms(x_ref[...], g_ref[...]).astype(o_ref.dtype)


def _rmsnorm(x, g, *, tm=512):
    t, d = x.shape
    return pl.pallas_call(
        _rmsnorm_body,
        grid=(t // tm,),
        in_specs=[pl.BlockSpec((tm, d), lambda i: (i, 0)),
                  pl.BlockSpec((1, d), lambda i: (0, 0))],
        out_specs=pl.BlockSpec((tm, d), lambda i: (i, 0)),
        out_shape=jax.ShapeDtypeStruct((t, d), BF16),
        compiler_params=_params(("arbitrary",)),
        name="rmsnorm",
    )(x, g.reshape(1, d))


def _mm_body(a_ref, w_ref, o_ref):
    acc = lax.dot_general(a_ref[...], w_ref[...].astype(BF16), _NN, preferred_element_type=F32)
    o_ref[...] = acc.astype(o_ref.dtype)


def _matmul(a, w, *, col0, ncols, tm, tn, out_dtype, name):
    t, k = a.shape
    return pl.pallas_call(
        _mm_body,
        grid=(t // tm, ncols // tn),
        in_specs=[pl.BlockSpec((tm, k), lambda i, j: (i, 0), pipeline_mode=pl.Buffered(1)),
                  pl.BlockSpec((k, tn), lambda i, j: (0, j + col0 // tn))],
        out_specs=pl.BlockSpec((tm, tn), lambda i, j: (i, j)),
        out_shape=jax.ShapeDtypeStruct((t, ncols), out_dtype),
        compiler_params=_params(("arbitrary", "arbitrary")),
        name=name,
    )(a, w)


def _mm_t_body(a_ref, wt_ref, o_ref, *, act):
    acc = lax.dot_general(a_ref[...], wt_ref[...].astype(BF16), _NT, preferred_element_type=F32)
    if act == "sigmoid":
        acc = jax.nn.sigmoid(acc)
    o_ref[...] = acc.astype(o_ref.dtype)


def _matmul_t(a, w_t, *, row0, ncols, tm, tn, out_dtype, act=None, name):
    t, k = a.shape
    return pl.pallas_call(
        functools.partial(_mm_t_body, act=act),
        grid=(t // tm, ncols // tn),
        in_specs=[pl.BlockSpec((tm, k), lambda i, j: (i, 0), pipeline_mode=pl.Buffered(1)),
                  pl.BlockSpec((pl.Element(tn), pl.Element(k)),
                               lambda i, j: (pl.multiple_of(row0 + j * tn, 8), 0))],
        out_specs=pl.BlockSpec((tm, tn), lambda i, j: (i, j)),
        out_shape=jax.ShapeDtypeStruct((t, ncols), out_dtype),
        compiler_params=_params(("arbitrary", "arbitrary")),
        name=name,
    )(a, w_t)


def _qproj_body(a_ref, w_ref, cos_ref, sin_ref, o_ref, *, first_rope_block, tn):
    acc = lax.dot_general(a_ref[...], w_ref[...].astype(BF16), _NN, preferred_element_type=F32)
    j = pl.program_id(1)

    @pl.when(j < first_rope_block)
    def _():
        o_ref[...] = acc.astype(o_ref.dtype)

    @pl.when(j >= first_rope_block)
    def _():
        reps = tn // 128
        c = jnp.concatenate([cos_ref[...]] * reps, axis=1)
        s = jnp.concatenate([sin_ref[...]] * reps, axis=1)
        lane = lax.broadcasted_iota(I32, acc.shape, 1)
        first = (lane & (MLA_ROPE_DIM - 1)) < MLA_ROPE_DIM // 2
        half = MLA_ROPE_DIM // 2
        partner = jnp.where(first, pltpu.roll(acc, tn - half, 1), pltpu.roll(acc, half, 1))
        o_ref[...] = (acc * c + partner * s).astype(o_ref.dtype)


def _qproj(a, w, cos, sin, *, rope_col0, tm, tn):
    t, k = a.shape
    n = w.shape[1]
    return pl.pallas_call(
        functools.partial(_qproj_body, first_rope_block=rope_col0 // tn, tn=tn),
        grid=(t // tm, n // tn),
        in_specs=[pl.BlockSpec((tm, k), lambda i, j: (i, 0), pipeline_mode=pl.Buffered(1)),
                  pl.BlockSpec((k, tn), lambda i, j: (0, j)),
                  pl.BlockSpec((tm, 128), lambda i, j: (i, 0)),
                  pl.BlockSpec((tm, 128), lambda i, j: (i, 0))],
        out_specs=pl.BlockSpec((tm, tn), lambda i, j: (i, j)),
        out_shape=jax.ShapeDtypeStruct((t, n), BF16),
        compiler_params=_params(("arbitrary", "arbitrary")),
        name="q_proj_rope",
    )(a, w, cos, sin)


def _merge_body(oa_ref, ob_ref, wa_ref, wb_ref, ga_ref, gb_ref, o_ref):
    pa = lax.dot_general(oa_ref[...], wa_ref[...].astype(BF16), _NN, preferred_element_type=F32)
    pb = lax.dot_general(ob_ref[...], wb_ref[...].astype(BF16), _NN, preferred_element_type=F32)
    o_ref[...] = (ga_ref[...] * pa + gb_ref[...] * pb).astype(o_ref.dtype)


def _merge(oa, ob, wa, wb, gates, *, tm, tn):
    t, k = oa.shape
    n = wa.shape[1]
    nb = n // tn
    return pl.pallas_call(
        _merge_body,
        grid=(t // tm, nb),
        in_specs=[pl.BlockSpec((tm, k), lambda i, j: (i, 0), pipeline_mode=pl.Buffered(1)),
                  pl.BlockSpec((tm, k), lambda i, j: (i, 0), pipeline_mode=pl.Buffered(1)),
                  pl.BlockSpec((k, tn), lambda i, j: (0, j)),
                  pl.BlockSpec((k, tn), lambda i, j: (0, j)),
                  pl.BlockSpec((tm, tn), lambda i, j: (i, j)),
                  pl.BlockSpec((tm, tn), lambda i, j: (i, j + nb))],
        out_specs=pl.BlockSpec((tm, tn), lambda i, j: (i, j)),
        out_shape=jax.ShapeDtypeStruct((t, n), BF16),
        compiler_params=_params(("arbitrary", "arbitrary")),
        name="gated_merge",
    )(oa, ob, wa, wb, gates, gates)


def _outproj_body(a_ref, w_ref, r_ref, o_ref):
    acc = lax.dot_general(a_ref[...], w_ref[...].astype(BF16), _NN, preferred_element_type=F32)
    o_ref[...] = r_ref[...] + acc


def _outproj(a, w, resid, *, tm, tn):
    t, k = a.shape
    n = w.shape[1]
    return pl.pallas_call(
        _outproj_body,
        grid=(t // tm, n // tn),
        in_specs=[pl.BlockSpec((tm, k), lambda i, j: (i, 0), pipeline_mode=pl.Buffered(1)),
                  pl.BlockSpec((k, tn), lambda i, j: (0, j)),
                  pl.BlockSpec((tm, tn), lambda i, j: (i, j))],
        out_specs=pl.BlockSpec((tm, tn), lambda i, j: (i, j)),
        out_shape=jax.ShapeDtypeStruct((t, n), F32),
        compiler_params=_params(("arbitrary", "arbitrary")),
        name="out_proj_residual",
    )(a, w, resid)


def _mla_prep_body(lat_ref, pos_ref, qn_ref, kvn_ref, invf_ref,
                   cq_ref, ckv_ref, kpe_ref, cos_ref, sin_ref, *, q_rank, kv_rank):
    cq_ref[...] = _rms(lat_ref[:, 0:q_rank], qn_ref[...]).astype(cq_ref.dtype)
    ckv_ref[...] = _rms(lat_ref[:, q_rank:q_rank + kv_rank], kvn_ref[...]).astype(ckv_ref.dtype)
    kr = lat_ref[:, q_rank + kv_rank:q_rank + kv_rank + 128]
    ang = pos_ref[...].astype(F32) * invf_ref[...]
    c = jnp.cos(ang)
    s = jnp.sin(ang)
    lane = lax.broadcasted_iota(I32, ang.shape, 1)
    half = MLA_ROPE_DIM // 2
    first = (lane & (MLA_ROPE_DIM - 1)) < half
    s = jnp.where(first, -s, s)
    partner = jnp.where(first, pltpu.roll(kr, 128 - half, 1), pltpu.roll(kr, half, 1))
    kpe = jnp.where(lane < MLA_ROPE_DIM, kr * c + partner * s, 0.0)
    kpe_ref[:, 0:128] = kpe.astype(kpe_ref.dtype)
    kpe_ref[:, 128:256] = pltpu.roll(kpe, MLA_ROPE_DIM, 1).astype(kpe_ref.dtype)
    cos_ref[...] = c
    sin_ref[...] = s


def _mla_prep(lat, positions, q_norm, kv_norm, inv_freq128, *, tm=512):
    t = lat.shape[0]
    q_rank, kv_rank = q_norm.shape[0], kv_norm.shape[0]
    return pl.pallas_call(
        functools.partial(_mla_prep_body, q_rank=q_rank, kv_rank=kv_rank),
        grid=(t // tm,),
        in_specs=[pl.BlockSpec((tm, lat.shape[1]), lambda i: (i, 0)),
                  pl.BlockSpec((tm, 1), lambda i: (i, 0)),
                  pl.BlockSpec((1, q_rank), lambda i: (0, 0)),
                  pl.BlockSpec((1, kv_rank), lambda i: (0, 0)),
                  pl.BlockSpec((1, 128), lambda i: (0, 0))],
        out_specs=[pl.BlockSpec((tm, q_rank), lambda i: (i, 0)),
                   pl.BlockSpec((tm, kv_rank), lambda i: (i, 0)),
                   pl.BlockSpec((tm, 256), lambda i: (i, 0)),
                   pl.BlockSpec((tm, 128), lambda i: (i, 0)),
                   pl.BlockSpec((tm, 128), lambda i: (i, 0))],
        out_shape=[jax.ShapeDtypeStruct((t, q_rank), BF16),
                   jax.ShapeDtypeStruct((t, kv_rank), BF16),
                   jax.ShapeDtypeStruct((t, 256), BF16),
                   jax.ShapeDtypeStruct((t, 128), F32),
                   jax.ShapeDtypeStruct((t, 128), F32)],
        compiler_params=_params(("arbitrary",)),
        name="mla_prep",
    )(lat, positions.reshape(t, 1), q_norm.reshape(1, -1), kv_norm.reshape(1, -1), inv_freq128)


def _mla_body(qn_ref, qp_ref, kv_ref, kpe_ref, o_ref, kf_ref, vt_ref, *, scale):
    qi = pl.program_id(2)

    @pl.when(qi == 0)
    def _():
        for hh in range(2):
            kf_ref[hh, :, 0:128] = kv_ref[:, 256 * hh:256 * hh + 128]
            kf_ref[hh, :, 128:256] = kpe_ref[:, 128 * hh:128 * hh + 128]
            v = kv_ref[:, 256 * hh + 128:256 * hh + 256]
            vt_ref[hh] = v.astype(F32).T.astype(BF16)

    lane = lax.broadcasted_iota(I32, qp_ref.shape, 1)
    for hh in range(2):
        qp = qp_ref[...]
        keep = (lane >= MLA_ROPE_DIM) if hh else (lane < MLA_ROPE_DIM)
        qp = jnp.where(keep, qp, jnp.zeros_like(qp))
        qf = jnp.concatenate([qn_ref[:, 128 * hh:128 * hh + 128], qp], axis=1)
        st = lax.dot_general(kf_ref[hh], qf, _NT, preferred_element_type=F32)
        m = jnp.max(st, axis=0, keepdims=True)
        e = jnp.exp2((st - m) * (scale * LOG2_E))
        l = jnp.sum(e, axis=0, keepdims=True)
        ot = lax.dot_general(vt_ref[hh], e.astype(BF16), _NN, preferred_element_type=F32)
        o_ref[:, 128 * hh:128 * hh + 128] = (ot / l).T.astype(o_ref.dtype)


def _mla_attention(q2, kv, kpe, *, batch, seq, tq=512):
    t = q2.shape[0]
    nq = seq // tq
    hp = MLA_HEADS // 2
    nope_w = MLA_HEADS * MLA_NOPE_DIM
    scale = float((MLA_NOPE_DIM + MLA_ROPE_DIM) ** -0.5)
    return pl.pallas_call(
        functools.partial(_mla_body, scale=scale),
        grid=(batch, hp, nq),
        in_specs=[pl.BlockSpec((tq, 256), lambda b, h, q: (b * nq + q, h)),
                  pl.BlockSpec((tq, 128), lambda b, h, q: (b * nq + q, nope_w // 128 + h)),
                  pl.BlockSpec((seq, 512), lambda b, h, q: (b, h)),
                  pl.BlockSpec((seq, 256), lambda b, h, q: (b, 0))],
        out_specs=pl.BlockSpec((tq, 256), lambda b, h, q: (b * nq + q, h)),
        out_shape=jax.ShapeDtypeStruct((t, MLA_HEADS * MLA_V_DIM), BF16),
        scratch_shapes=[pltpu.VMEM((2, seq, 256), BF16), pltpu.VMEM((2, 128, seq), BF16)],
        compiler_params=_params(("arbitrary", "arbitrary", "arbitrary")),
        name="mla_attention",
    )(q2, q2, kv, kpe)


NA_Q_ROWS = 4
NA_K_ROWS = 12
NA_PAD_BLOCKS = NA_Q_ROWS
NA_BIAS_BLOCKS = 2 * NA_WIN_ROWS - 1 + 2 * NA_PAD_BLOCKS + 1


def _na_group(g, rows):
    r0 = NA_Q_ROWS * g
    w0 = min(max(r0 - NA_WIN_ROWS // 2, 0), rows - NA_K_ROWS)
    return r0, w0


def _na_bias_tables(rpb):
    n_heads = rpb.shape[0]
    kj = np.arange(GRID_W)[:, None]
    c = np.arange(GRID_W)[None, :]
    cs = np.clip(c - NA_WIN_COLS // 2, 0, GRID_W - NA_WIN_COLS)
    col_ok = (kj >= cs) & (kj < cs + NA_WIN_COLS)
    dc = kj - c + NA_WIN_COLS - 1
    pick = np.stack([(dc == dd) & col_ok for dd in range(2 * NA_WIN_COLS - 1)]).astype(np.float32)
    toep = jnp.einsum("hrd,dkc->hrkc", rpb.astype(F32), pick, precision=lax.Precision.HIGHEST)
    toep = jnp.where(col_ok[None, None], toep, NEG_BIG)
    n_off = 2 * NA_WIN_ROWS - 1
    blocks = jnp.pad(toep[:, ::-1], ((0, 0), (NA_PAD_BLOCKS, NA_BIAS_BLOCKS - NA_PAD_BLOCKS - n_off), (0, 0), (0, 0)),
                     constant_values=NEG_BIG)
    even = blocks.transpose(0, 2, 1, 3).reshape(n_heads, GRID_W, NA_BIAS_BLOCKS * GRID_W)
    odd = jnp.pad(even[:, :, GRID_W:], ((0, 0), (0, 0), (0, GRID_W)), constant_values=NEG_BIG)
    return even, odd


def _na_group_bias(even_ref, odd_ref, g, rows):
    r0, w0 = _na_group(g, rows)
    nq = NA_Q_ROWS * GRID_W
    slabs = []
    for k in range(NA_K_ROWS):
        ki = w0 + k
        bad = []
        for q in range(NA_Q_ROWS):
            rs = min(max(r0 + q - NA_WIN_ROWS // 2, 0), rows - NA_WIN_ROWS)
            if not rs <= ki < rs + NA_WIN_ROWS:
                bad.append(q)
        if len(bad) == NA_Q_ROWS:
            slabs.append(jnp.full((GRID_W, nq), NEG_BIG, F32))
            continue
        i0 = NA_PAD_BLOCKS + (NA_WIN_ROWS - 1) - (ki - r0)
        ref, first = (even_ref, i0) if i0 % 2 == 0 else (odd_ref, i0 - 1)
        slab = ref[0, :, first * GRID_W:first * GRID_W + nq]
        if bad:
            q_of_lane = lax.broadcasted_iota(I32, slab.shape, 1) // GRID_W
            outside = q_of_lane == bad[0]
            for q in bad[1:]:
                outside = outside | (q_of_lane == q)
            slab = jnp.where(outside, NEG_BIG, slab)
        slabs.append(slab)
    return jnp.concatenate(slabs, axis=0)


def _na_body(q_ref, k_ref, v_ref, even_ref, odd_ref, o_ref, *, rows, scale):
    vt = v_ref[...].astype(F32).T.astype(BF16)
    nq, nk = NA_Q_ROWS * GRID_W, NA_K_ROWS * GRID_W
    for g in range(rows // NA_Q_ROWS):
        r0, w0 = _na_group(g, rows)
        kwin = k_ref[w0 * GRID_W:w0 * GRID_W + nk, :]
        qg = q_ref[r0 * GRID_W:r0 * GRID_W + nq, :]
        st = (lax.dot_general(kwin, qg, _NT, preferred_element_type=F32) * scale
              + _na_group_bias(even_ref, odd_ref, g, rows))
        m = jnp.max(st, axis=0, keepdims=True)
        e = jnp.exp(st - m)
        l = jnp.sum(e, axis=0, keepdims=True)
        ot = lax.dot_general(vt[:, w0 * GRID_W:w0 * GRID_W + nk], e.astype(BF16), _NN,
                             preferred_element_type=F32)
        o_ref[r0 * GRID_W:r0 * GRID_W + nq, :] = (ot / l).T.astype(o_ref.dtype)


def _na_attention(qkv, bias_even, bias_odd, *, batch, seq):
    t = qkv.shape[0]
    rows = seq // GRID_W
    scale = float(NA_HEAD_DIM ** -0.5)
    bias_spec = pl.BlockSpec((1, GRID_W, NA_BIAS_BLOCKS * GRID_W), lambda h, b: (h, 0, 0))
    return pl.pallas_call(
        functools.partial(_na_body, rows=rows, scale=scale),
        grid=(NA_HEADS, batch),
        in_specs=[pl.BlockSpec((seq, NA_HEAD_DIM), lambda h, b: (b, h)),
                  pl.BlockSpec((seq, NA_HEAD_DIM), lambda h, b: (b, NA_HEADS + h)),
                  pl.BlockSpec((seq, NA_HEAD_DIM), lambda h, b: (b, 2 * NA_HEADS + h)),
                  bias_spec, bias_spec],
        out_specs=pl.BlockSpec((seq, NA_HEAD_DIM), lambda h, b: (b, h)),
        out_shape=jax.ShapeDtypeStruct((t, NA_HEADS * NA_HEAD_DIM), BF16),
        compiler_params=_params(("arbitrary", "arbitrary")),
        name="na_attention",
    )(qkv, qkv, qkv, bias_even, bias_odd)


def _router_body(x_ref, g_ref, wr_ref, br_ref, hn_ref, idx_ref, wts_ref, rank_ref, cnt_ref, carry_ref,
                 *, n_exp, tr):
    i = pl.program_id(0)

    @pl.when(i == 0)
    def _():
        carry_ref[...] = jnp.zeros_like(carry_ref)

    y = _rms(x_ref[...], g_ref[...])
    yb = y.astype(BF16)
    hn_ref[...] = _pack_bf16_pairs(y)

    logits = lax.dot_general(wr_ref[...].astype(BF16), yb, _NT, preferred_element_type=F32) + br_ref[...]
    eid = lax.broadcasted_iota(I32, (n_exp, tr), 0).astype(F32)
    work = logits
    vals, sels = [], []
    for k in range(TOP_K):
        m = jnp.max(work, axis=0, keepdims=True)
        first = jnp.min(jnp.where(work == m, eid, float(n_exp)), axis=0, keepdims=True)
        sel = eid == first
        vals.append(m)
        sels.append(sel)
        idx_ref[k:k + 1, :] = first.astype(I32)
        work = jnp.where(sel, -jnp.inf, work)
    es = [jnp.exp(v - vals[0]) for v in vals]
    denom = es[0] + es[1] + es[2] + es[3]
    for k in range(TOP_K):
        wts_ref[k:k + 1, :] = es[k] / denom

    chosen = jnp.zeros((n_exp, tr), F32)
    for sel in sels:
        chosen = chosen + sel.astype(F32)
    before = (lax.broadcasted_iota(I32, (tr, tr), 0) < lax.broadcasted_iota(I32, (tr, tr), 1)).astype(BF16)
    carry = carry_ref[:, 0:1]
    base = lax.dot_general(chosen.astype(BF16), before, _NN, preferred_element_type=F32) + carry
    for k in range(TOP_K):
        rank_ref[k:k + 1, :] = jnp.sum(jnp.where(sels[k], base, 0.0), axis=0, keepdims=True).astype(I32)
    total = carry + jnp.sum(chosen, axis=1, keepdims=True)
    carry_ref[...] = jnp.broadcast_to(total, carry_ref.shape)
    cnt_ref[...] = jnp.broadcast_to(total, cnt_ref.shape).astype(I32)


def _router(x1, g, wr_t, br, *, tr=512):
    t, d = x1.shape
    n_exp = wr_t.shape[0]
    return pl.pallas_call(
        functools.partial(_router_body, n_exp=n_exp, tr=tr),
        grid=(t // tr,),
        in_specs=[pl.BlockSpec((tr, d), lambda i: (i, 0)),
                  pl.BlockSpec((1, d), lambda i: (0, 0)),
                  pl.BlockSpec((n_exp, d), lambda i: (0, 0)),
                  pl.BlockSpec((n_exp, 1), lambda i: (0, 0))],
        out_specs=[pl.BlockSpec((tr, d // 2), lambda i: (i, 0)),
                   pl.BlockSpec((TOP_K, tr), lambda i: (0, i)),
                   pl.BlockSpec((TOP_K, tr), lambda i: (0, i)),
                   pl.BlockSpec((TOP_K, tr), lambda i: (0, i)),
                   pl.BlockSpec((n_exp, 128), lambda i: (0, 0))],
        out_shape=[jax.ShapeDtypeStruct((t, d // 2), U32),
                   jax.ShapeDtypeStruct((TOP_K, t), I32),
                   jax.ShapeDtypeStruct((TOP_K, t), F32),
                   jax.ShapeDtypeStruct((TOP_K, t), I32),
                   jax.ShapeDtypeStruct((n_exp, 128), I32)],
        scratch_shapes=[pltpu.VMEM((n_exp, 128), F32)],
        compiler_params=_params(("arbitrary",)),
        name="router_topk",
    )(x1, g.reshape(1, d), wr_t, br.reshape(n_exp, 1))


def _dispatch_body(pos_ref, hn_ref, xg_ref, sem, *, td):
    def row_copy(t, k):
        return pltpu.make_async_copy(hn_ref.at[pl.ds(t, 1), :],
                                     xg_ref.at[pl.ds(pos_ref[k, t], 1), :], sem)

    def start(t, c):
        for k in range(TOP_K):
            row_copy(t, k).start(priority=k % 2)
        return c

    def wait(t, c):
        for k in range(TOP_K):
            row_copy(t, k).wait()
        return c

    lax.fori_loop(0, td, start, 0)
    lax.fori_loop(0, td, wait, 0)


def _dispatch(pos, hn, n_rows, *, td=512):
    t = hn.shape[0]
    return pl.pallas_call(
        functools.partial(_dispatch_body, td=td),
        grid=(t // td,),
        in_specs=[pl.BlockSpec((TOP_K, td), lambda i: (0, i), memory_space=pltpu.SMEM),
                  pl.BlockSpec((td, hn.shape[1]), lambda i: (i, 0))],
        out_specs=pl.BlockSpec(memory_space=pl.ANY),
        out_shape=jax.ShapeDtypeStruct((n_rows, hn.shape[1]), hn.dtype),
        scratch_shapes=[pltpu.SemaphoreType.DMA(())],
        compiler_params=_params(("arbitrary",)),
        name="moe_dispatch",
    )(pos, hn)


def _expert_up_body(ie_ref, ib_ref, ins_ref, inr_ref, x_ref, w_ref, b_ref, o_ref, *, tn):
    w = pl.program_id(0)
    nsub = ins_ref[w]
    nrows = inr_ref[w]
    half = w_ref.shape[0] // 2

    for n in range(1, ITEM_SUBS + 1):
        @pl.when(nsub == n)
        def _(n=n):
            m = n * SUB
            row = lax.broadcasted_iota(I32, (m, 1), 0)
            xw = jnp.where(row < nrows, x_ref[0:m, :], jnp.uint32(0))
            lo, hi = _unpack_bf16_pairs(xw)
            lo, hi = lo.astype(BF16), hi.astype(BF16)
            hw = tn // 2

            def gate_up(c0):
                return (lax.dot_general(lo, w_ref[0:half, c0:c0 + hw].astype(BF16), _NN,
                                        preferred_element_type=F32)
                        + lax.dot_general(hi, w_ref[half:, c0:c0 + hw].astype(BF16), _NN,
                                          preferred_element_type=F32)
                        + b_ref[:, c0:c0 + hw])

            def swiglu(gate, up):
                gate = jnp.minimum(gate, SWIGLU_LIMIT)
                up = jnp.clip(up, -SWIGLU_LIMIT, SWIGLU_LIMIT)
                return (up + 1.0) * gate * jax.nn.sigmoid(SWIGLU_ALPHA * gate)

            first, second = gate_up(0), gate_up(hw)
            act_first = swiglu(first, pltpu.roll(first, hw - 1, 1))
            act_second = swiglu(pltpu.roll(second, 1, 1), second)
            lane = lax.broadcasted_iota(I32, (m, hw), 1)
            o_ref[0:m, :] = jnp.where((lane & 1) == 0, act_first, act_second).astype(o_ref.dtype)
            if m < ITEM_ROWS:
                o_ref[m:, :] = jnp.zeros((ITEM_ROWS - m, tn // 2), o_ref.dtype)


def _expert_up(items, n_items, xg, w_gu, b_gu, *, tn):
    item_e, item_blk, item_nsub, item_rows = items
    n_exp, d, f2 = w_gu.shape
    nj = f2 // tn
    n_rows = xg.shape[0]

    def jmap(j, ns, w):
        return jnp.where(ns[w] > 0, j, nj - 1)

    grid_spec = pltpu.PrefetchScalarGridSpec(
        num_scalar_prefetch=4,
        grid=(n_items, nj),
        in_specs=[pl.BlockSpec((ITEM_ROWS, xg.shape[1]), lambda w, j, ie, ib, ns, nr: (ib[w], 0)),
                  pl.BlockSpec((None, d, tn), lambda w, j, ie, ib, ns, nr: (ie[w], 0, jmap(j, ns, w))),
                  pl.BlockSpec((None, 1, tn), lambda w, j, ie, ib, ns, nr: (ie[w], 0, jmap(j, ns, w)))],
        out_specs=pl.BlockSpec((ITEM_ROWS, tn // 2), lambda w, j, ie, ib, ns, nr: (ib[w], jmap(j, ns, w))),
    )
    return pl.pallas_call(
        functools.partial(_expert_up_body, tn=tn),
        grid_spec=grid_spec,
        out_shape=jax.ShapeDtypeStruct((n_rows, f2 // 2), BF16),
        compiler_params=_params(("arbitrary", "arbitrary")),
        name="expert_gate_up",
    )(item_e, item_blk, item_nsub, item_rows, xg, w_gu, b_gu.reshape(n_exp, 1, f2))


def _interleave_rows_bf16(a, b):
    def rounded(x):
        bits = lax.bitcast_convert_type(x, U32)
        return bits + jnp.uint32(0x7FFF) + ((bits >> 16) & jnp.uint32(1))
    word = (rounded(b) & jnp.uint32(0xFFFF0000)) | (rounded(a) >> 16)
    return pltpu.bitcast(word, BF16)


def _expert_down_body(ie_ref, ib_ref, ins_ref, a_ref, w_ref, b_ref, o_ref, *, group):
    w = pl.program_id(0)
    nsub = ins_ref[w]

    for n in range(1, ITEM_SUBS + 1):
        @pl.when(nsub == n)
        def _(n=n):
            m = n * SUB
            wb = jnp.concatenate(
                [_interleave_rows_bf16(w_ref[g0:g0 + group // 2, :], w_ref[g0 + group // 2:g0 + group, :])
                 for g0 in range(0, w_ref.shape[0], group)], axis=0)
            y = lax.dot_general(a_ref[0:m, :], wb, _NN, preferred_element_type=F32)
            o_ref[0:m, :] = _pack_bf16_pairs(y + b_ref[...])
            if m < ITEM_ROWS:
                o_ref[m:, :] = jnp.zeros((ITEM_ROWS - m, o_ref.shape[1]), o_ref.dtype)


def _expert_down(items, n_items, act, w_d, b_d, *, group, tn):
    item_e, item_blk, item_nsub, _ = items
    n_exp, f, d = w_d.shape
    nj = d // tn
    n_rows = act.shape[0]

    def jmap(j, ns, w):
        return jnp.where(ns[w] > 0, j, nj - 1)

    grid_spec = pltpu.PrefetchScalarGridSpec(
        num_scalar_prefetch=3,
        grid=(n_items, nj),
        in_specs=[pl.BlockSpec((ITEM_ROWS, f), lambda w, j, ie, ib, ns: (ib[w], 0)),
                  pl.BlockSpec((None, f, tn), lambda w, j, ie, ib, ns: (ie[w], 0, jmap(j, ns, w))),
                  pl.BlockSpec((None, 1, tn), lambda w, j, ie, ib, ns: (ie[w], 0, jmap(j, ns, w)))],
        out_specs=pl.BlockSpec((ITEM_ROWS, tn // 2), lambda w, j, ie, ib, ns: (ib[w], jmap(j, ns, w))),
    )
    return pl.pallas_call(
        functools.partial(_expert_down_body, group=group),
        grid_spec=grid_spec,
        out_shape=jax.ShapeDtypeStruct((n_rows, d // 2), U32),
        compiler_params=_params(("arbitrary", "arbitrary")),
        name="expert_down",
    )(item_e, item_blk, item_nsub, act, w_d, b_d.reshape(n_exp, 1, d))


def _combine_body(pos_ref, posn_ref, wts_ref, x_ref, g_ref, yg_ref, o_ref, buf_ref, sem, *, tc, pw):
    i = pl.program_id(0)
    n = pl.num_programs(0)
    slot = lax.rem(i, 2)

    def row_copy(p_ref, s, t, k):
        return pltpu.make_async_copy(yg_ref.at[pl.ds(p_ref[k, t], 1), :],
                                     buf_ref.at[s, k, pl.ds(t, 1), :], sem.at[s])

    def start_tile(p_ref, s):
        def body(t, c):
            for k in range(TOP_K):
                row_copy(p_ref, s, t, k).start(priority=k % 2)
            return c
        lax.fori_loop(0, tc, body, 0)

    @pl.when(i == 0)
    def _():
        start_tile(pos_ref, 0)

    @pl.when(i + 1 < n)
    def _():
        start_tile(posn_ref, 1 - slot)

    def wait_body(t, c):
        for k in range(TOP_K):
            row_copy(pos_ref, slot, t, k).wait()
        return c
    lax.fori_loop(0, tc, wait_body, 0)

    parts = []
    for j in range(buf_ref.shape[3] // pw):
        lo = hi = None
        for k in range(TOP_K):
            wk = wts_ref[:, k:k + 1]
            l, h = _unpack_bf16_pairs(buf_ref[slot, k, :, j * pw:(j + 1) * pw])
            lo = wk * l if lo is None else lo + wk * l
            hi = wk * h if hi is None else hi + wk * h
        parts += [lo, hi]
    o_ref[...] = _rms(x_ref[...] + jnp.concatenate(parts, axis=1), g_ref[...])


def _combine(pos, wts_t, x1, g, yg, *, pw, tc=128):
    t, d = x1.shape
    nt = t // tc
    return pl.pallas_call(
        functools.partial(_combine_body, tc=tc, pw=pw),
        grid=(nt,),
        in_specs=[pl.BlockSpec((TOP_K, tc), lambda i: (0, i), memory_space=pltpu.SMEM),
                  pl.BlockSpec((TOP_K, tc), lambda i: (0, jnp.minimum(i + 1, nt - 1)), memory_space=pltpu.SMEM),
                  pl.BlockSpec((tc, TOP_K), lambda i: (i, 0)),
                  pl.BlockSpec((tc, d), lambda i: (i, 0)),
                  pl.BlockSpec((1, d), lambda i: (0, 0)),
                  pl.BlockSpec(memory_space=pl.ANY)],
        out_specs=pl.BlockSpec((tc, d), lambda i: (i, 0)),
        out_shape=jax.ShapeDtypeStruct((t, d), F32),
        scratch_shapes=[pltpu.VMEM((2, TOP_K, tc, d // 2), U32), pltpu.SemaphoreType.DMA((2,))],
        compiler_params=_params(("arbitrary",)),
        name="moe_combine_norm",
    )(pos, pos, wts_t, x1, g.reshape(1, d), yg)


def _plan_items(counts, n_assign):
    n_exp = counts.shape[0]
    max_items = n_assign // ITEM_ROWS + n_exp
    nsub_e = (counts + SUB - 1) // SUB
    nitem_e = (counts + ITEM_ROWS - 1) // ITEM_ROWS
    last_item_e = jnp.cumsum(nitem_e)
    first_item_e = last_item_e - nitem_e
    total = last_item_e[-1]
    w = jnp.arange(max_items, dtype=I32)
    valid = w < total
    e_w = jnp.minimum(jnp.searchsorted(last_item_e, w, side="right"), n_exp - 1).astype(I32)
    e_last = e_w[jnp.maximum(total - 1, 0)]
    e_w = jnp.where(valid, e_w, e_last)
    c_w = w - first_item_e[e_w]
    nsub_w = jnp.where(valid, jnp.clip(nsub_e[e_w] - ITEM_SUBS * c_w, 0, ITEM_SUBS), 0).astype(I32)
    rows_w = jnp.where(valid, jnp.clip(counts[e_w] - ITEM_ROWS * c_w, 0, ITEM_ROWS), 0).astype(I32)
    blk_w = jnp.where(valid, w, max_items).astype(I32)
    row_off_e = (first_item_e * ITEM_ROWS).astype(I32)
    return (e_w, blk_w, nsub_w, rows_w), total.astype(I32), row_off_e, (max_items + 1) * ITEM_ROWS


def kernel(x, positions, norm_mix, w_in, q_a_norm, w_q_b, kv_a_norm, w_kv_b, na_rpb, w_proj_a, w_proj_b, w_out, norm_ffn, w_router, b_router, w_gate_up, b_gate_up, w_down, b_down, norm_final):
    b, s, d = x.shape
    t = b * s
    na_w = NA_HEADS * NA_HEAD_DIM
    q_rank, kv_rank = q_a_norm.shape[1], kv_a_norm.shape[1]
    lat0 = 3 * na_w
    gate0 = lat0 + q_rank + kv_rank + MLA_ROPE_DIM
    xf = x.reshape(t, d)
    w_in_t = jnp.swapaxes(w_in, 1, 2)[0]

    hn = _rmsnorm(xf, norm_mix[0])
    qkv = _matmul_t(hn, w_in_t, row0=0, ncols=lat0, tm=2048, tn=512, out_dtype=BF16, name="proj_qkv")
    lat = _matmul_t(hn, w_in_t, row0=lat0, ncols=2048, tm=2048, tn=512, out_dtype=F32, name="proj_latent")
    gates = _matmul_t(hn, w_in_t, row0=gate0, ncols=2 * d, tm=2048, tn=512, out_dtype=BF16,
                      act="sigmoid", name="proj_gates")

    oa = _na_attention(qkv, *_na_bias_tables(na_rpb[0]), batch=b, seq=s)

    half = MLA_ROPE_DIM // 2
    inv_freq = ROPE_THETA ** (-(jnp.arange(half, dtype=F32) * 2.0) / MLA_ROPE_DIM)
    inv_freq128 = jnp.tile(inv_freq, 128 // half).reshape(1, 128)
    cqn, ckvn, kpe, cos, sin = _mla_prep(lat, positions.reshape(t), q_a_norm[0], kv_a_norm[0], inv_freq128)
    qk_dim = MLA_NOPE_DIM + MLA_ROPE_DIM
    wq = w_q_b[0].reshape(q_rank, MLA_HEADS, qk_dim)
    wq = jnp.concatenate([wq[:, :, :MLA_NOPE_DIM].reshape(q_rank, -1),
                          wq[:, :, MLA_NOPE_DIM:].reshape(q_rank, -1)], axis=1)
    q2 = _qproj(cqn, wq, cos, sin, rope_col0=MLA_HEADS * MLA_NOPE_DIM, tm=2048, tn=512)
    kv = _matmul(ckvn, w_kv_b[0], col0=0, ncols=w_kv_b.shape[2], tm=2048, tn=1024, out_dtype=BF16,
                 name="kv_proj")
    ob = _mla_attention(q2, kv, kpe, batch=b, seq=s)

    y = _merge(oa, ob, w_proj_a[0], w_proj_b[0], gates, tm=2048, tn=512)
    x1 = _outproj(y, w_out[0], xf, tm=2048, tn=512)

    hn_packed, idx, wts, rank, counts = _router(x1, norm_ffn[0], w_router[0].T, b_router[0])
    items, n_items, row_off, n_rows = _plan_items(counts[:, 0], t * TOP_K)
    onehot = idx[None] == jnp.arange(row_off.shape[0], dtype=I32)[:, None, None]
    pos = jnp.sum(jnp.where(onehot, row_off[:, None, None], 0), axis=0) + rank
    xg = _dispatch(pos, hn_packed, n_rows)
    up_tn, down_tn = 512, 1024
    act = _expert_up(items, n_items, xg, w_gate_up[0], b_gate_up[0], tn=up_tn)
    yg = _expert_down(items, n_items, act, w_down[0], b_down[0], group=up_tn // 2, tn=down_tn)
    out = _combine(pos, wts.T, x1, norm_final, yg, pw=down_tn // 2)
    return out.reshape(b, s, d)
```

```python
import functools

import numpy as np
import jax
import jax.numpy as jnp
from jax import lax
from jax.experimental import pallas as pl
from jax.experimental.pallas import tpu as pltpu

F32 = jnp.float32
BF16 = jnp.bfloat16
U32 = jnp.uint32
I32 = jnp.int32

GRID_W = 64
NA_HEADS = 16
NA_HEAD_DIM = 128
NA_WIN_ROWS = 8
NA_WIN_COLS = 16
MLA_HEADS = 16
MLA_NOPE_DIM = 128
MLA_ROPE_DIM = 64
MLA_V_DIM = 128
ROPE_THETA = 10000.0
TOP_K = 4
SWIGLU_LIMIT = 7.0
SWIGLU_ALPHA = 1.702
NORM_EPS = 1e-6

NEG_BIG = -1e30
LOG2_E = 1.4426950408889634

V7X_VMEM_BYTES = 64 * 1024 * 1024
VMEM_LIMIT = V7X_VMEM_BYTES - 4 * 1024 * 1024

SUB = 128
ITEM_SUBS = 10
ITEM_ROWS = SUB * ITEM_SUBS
DMA_UNROLL = 8

_NN = (((1,), (0,)), ((), ()))
_NT = (((1,), (1,)), ((), ()))


def _params(sem=None):
    return pltpu.CompilerParams(vmem_limit_bytes=VMEM_LIMIT, dimension_semantics=sem)


def _rms(x, g):
    return x * lax.rsqrt(jnp.mean(x * x, axis=-1, keepdims=True) + NORM_EPS) * g


def _pack_bf16_pairs(x):
    n = x.shape[1] // 2
    lo = lax.bitcast_convert_type(x[:, :n].astype(BF16).astype(F32), U32)
    hi = lax.bitcast_convert_type(x[:, n:].astype(BF16).astype(F32), U32)
    return (hi & jnp.uint32(0xFFFF0000)) | (lo >> 16)


def _unpack_bf16_pairs(w):
    return (lax.bitcast_convert_type(w << 16, F32),
            lax.bitcast_convert_type(w & jnp.uint32(0xFFFF0000), F32))


def _rmsnorm_body(x_ref, g_ref, o_ref):
    o_ref[...] = _rms(x_ref[...], g_ref[...]).astype(o_ref.dtype)


def _rmsnorm(x, g, *, tm=512):
    t, d = x.shape
    return pl.pallas_call(
        _rmsnorm_body,
        grid=(t // tm,),
        in_specs=[pl.BlockSpec((tm, d), lambda i: (i, 0)),
                  pl.BlockSpec((1, d), lambda i: (0, 0))],
        out_specs=pl.BlockSpec((tm, d), lambda i: (i, 0)),
        out_shape=jax.ShapeDtypeStruct((t, d), BF16),
        compiler_params=_params(("arbitrary",)),
        name="rmsnorm",
    )(x, g.reshape(1, d))


def _mm_body(a_ref, w_ref, o_ref):
    acc = lax.dot_general(a_ref[...], w_ref[...].astype(BF16), _NN, preferred_element_type=F32)
    o_ref[...] = acc.astype(o_ref.dtype)


def _matmul(a, w, *, col0, ncols, tm, tn, out_dtype, name):
    t, k = a.shape
    return pl.pallas_call(
        _mm_body,
        grid=(t // tm, ncols // tn),
        in_specs=[pl.BlockSpec((tm, k), lambda i, j: (i, 0), pipeline_mode=pl.Buffered(1)),
                  pl.BlockSpec((k, tn), lambda i, j: (0, j + col0 // tn))],
        out_specs=pl.BlockSpec((tm, tn), lambda i, j: (i, j)),
        out_shape=jax.ShapeDtypeStruct((t, ncols), out_dtype),
        compiler_params=_params(("arbitrary", "arbitrary")),
        name=name,
    )(a, w)


def _mm_t_body(a_ref, wt_ref, o_ref):
    acc = lax.dot_general(a_ref[...], wt_ref[...].astype(BF16), _NT, preferred_element_type=F32)
    o_ref[...] = acc.astype(o_ref.dtype)


def _matmul_t(a, w_t, *, row0, ncols, tm, tn, out_dtype, name):
    t, k = a.shape
    return pl.pallas_call(
        _mm_t_body,
        grid=(t // tm, ncols // tn),
        in_specs=[pl.BlockSpec((tm, k), lambda i, j: (i, 0), pipeline_mode=pl.Buffered(1)),
                  pl.BlockSpec((pl.Element(tn), pl.Element(k)),
                               lambda i, j: (pl.multiple_of(row0 + j * tn, 8), 0))],
        out_specs=pl.BlockSpec((tm, tn), lambda i, j: (i, j)),
        out_shape=jax.ShapeDtypeStruct((t, ncols), out_dtype),
        compiler_params=_params(("arbitrary", "arbitrary")),
        name=name,
    )(a, w_t)


def _qproj_body(a_ref, w_ref, cos_ref, sin_ref, o_ref, *, first_rope_block, tn):
    acc = lax.dot_general(a_ref[...], w_ref[...].astype(BF16), _NN, preferred_element_type=F32)
    j = pl.program_id(1)

    @pl.when(j < first_rope_block)
    def _():
        o_ref[...] = acc.astype(o_ref.dtype)

    @pl.when(j >= first_rope_block)
    def _():
        reps = tn // 128
        c = jnp.concatenate([cos_ref[...]] * reps, axis=1)
        s = jnp.concatenate([sin_ref[...]] * reps, axis=1)
        lane = lax.broadcasted_iota(I32, acc.shape, 1)
        first = (lane & (MLA_ROPE_DIM - 1)) < MLA_ROPE_DIM // 2
        half = MLA_ROPE_DIM // 2
        partner = jnp.where(first, pltpu.roll(acc, tn - half, 1), pltpu.roll(acc, half, 1))
        o_ref[...] = (acc * c + partner * s).astype(o_ref.dtype)


def _qproj(a, w, cos, sin, *, rope_col0, tm, tn):
    t, k = a.shape
    n = w.shape[1]
    return pl.pallas_call(
        functools.partial(_qproj_body, first_rope_block=rope_col0 // tn, tn=tn),
        grid=(t // tm, n // tn),
        in_specs=[pl.BlockSpec((tm, k), lambda i, j: (i, 0), pipeline_mode=pl.Buffered(1)),
                  pl.BlockSpec((k, tn), lambda i, j: (0, j)),
                  pl.BlockSpec((tm, 128), lambda i, j: (i, 0)),
                  pl.BlockSpec((tm, 128), lambda i, j: (i, 0))],
        out_specs=pl.BlockSpec((tm, tn), lambda i, j: (i, j)),
        out_shape=jax.ShapeDtypeStruct((t, n), BF16),
        compiler_params=_params(("arbitrary", "arbitrary")),
        name="q_proj_rope",
    )(a, w, cos, sin)


def _merge_body(oa_ref, ob_ref, wa_ref, wb_ref, ga_ref, gb_ref, o_ref):
    pa = lax.dot_general(oa_ref[...], wa_ref[...].astype(BF16), _NN, preferred_element_type=F32)
    pb = lax.dot_general(ob_ref[...], wb_ref[...].astype(BF16), _NN, preferred_element_type=F32)
    o_ref[...] = (ga_ref[...] * pa + gb_ref[...] * pb).astype(o_ref.dtype)


def _merge(oa, ob, wa, wb, gates, *, tm, tn):
    t, k = oa.shape
    n = wa.shape[1]
    nb = n // tn
    return pl.pallas_call(
        _merge_body,
        grid=(t // tm, nb),
        in_specs=[pl.BlockSpec((tm, k), lambda i, j: (i, 0), pipeline_mode=pl.Buffered(1)),
                  pl.BlockSpec((tm, k), lambda i, j: (i, 0), pipeline_mode=pl.Buffered(1)),
                  pl.BlockSpec((k, tn), lambda i, j: (0, j)),
                  pl.BlockSpec((k, tn), lambda i, j: (0, j)),
                  pl.BlockSpec((tm, tn), lambda i, j: (i, j)),
                  pl.BlockSpec((tm, tn), lambda i, j: (i, j + nb))],
        out_specs=pl.BlockSpec((tm, tn), lambda i, j: (i, j)),
        out_shape=jax.ShapeDtypeStruct((t, n), BF16),
        compiler_params=_params(("arbitrary", "arbitrary")),
        name="gated_merge",
    )(oa, ob, wa, wb, gates, gates)


def _outproj_body(a_ref, w_ref, r_ref, o_ref):
    acc = lax.dot_general(a_ref[...], w_ref[...].astype(BF16), _NN, preferred_element_type=F32)
    o_ref[...] = r_ref[...] + acc


def _outproj(a, w, resid, *, tm, tn):
    t, k = a.shape
    n = w.shape[1]
    return pl.pallas_call(
        _outproj_body,
        grid=(t // tm, n // tn),
        in_specs=[pl.BlockSpec((tm, k), lambda i, j: (i, 0), pipeline_mode=pl.Buffered(1)),
                  pl.BlockSpec((k, tn), lambda i, j: (0, j)),
                  pl.BlockSpec((tm, tn), lambda i, j: (i, j))],
        out_specs=pl.BlockSpec((tm, tn), lambda i, j: (i, j)),
        out_shape=jax.ShapeDtypeStruct((t, n), F32),
        compiler_params=_params(("arbitrary", "arbitrary")),
        name="out_proj_residual",
    )(a, w, resid)


def _mla_prep_body(lat_ref, pos_ref, qn_ref, kvn_ref, invf_ref,
                   cq_ref, ckv_ref, kpe_ref, cos_ref, sin_ref, *, q_rank, kv_rank):
    cq_ref[...] = _rms(lat_ref[:, 0:q_rank], qn_ref[...]).astype(cq_ref.dtype)
    ckv_ref[...] = _rms(lat_ref[:, q_rank:q_rank + kv_rank], kvn_ref[...]).astype(ckv_ref.dtype)
    kr = lat_ref[:, q_rank + kv_rank:q_rank + kv_rank + 128]
    ang = pos_ref[...].astype(F32) * invf_ref[...]
    c = jnp.cos(ang)
    s = jnp.sin(ang)
    lane = lax.broadcasted_iota(I32, ang.shape, 1)
    half = MLA_ROPE_DIM // 2
    first = (lane & (MLA_ROPE_DIM - 1)) < half
    s = jnp.where(first, -s, s)
    partner = jnp.where(first, pltpu.roll(kr, 128 - half, 1), pltpu.roll(kr, half, 1))
    kpe = jnp.where(lane < MLA_ROPE_DIM, kr * c + partner * s, 0.0)
    kpe_ref[:, 0:128] = kpe.astype(kpe_ref.dtype)
    kpe_ref[:, 128:256] = pltpu.roll(kpe, MLA_ROPE_DIM, 1).astype(kpe_ref.dtype)
    cos_ref[...] = c
    sin_ref[...] = s


def _mla_prep(lat, positions, q_norm, kv_norm, inv_freq128, *, tm=512):
    t = lat.shape[0]
    q_rank, kv_rank = q_norm.shape[0], kv_norm.shape[0]
    return pl.pallas_call(
        functools.partial(_mla_prep_body, q_rank=q_rank, kv_rank=kv_rank),
        grid=(t // tm,),
        in_specs=[pl.BlockSpec((tm, lat.shape[1]), lambda i: (i, 0)),
                  pl.BlockSpec((tm, 1), lambda i: (i, 0)),
                  pl.BlockSpec((1, q_rank), lambda i: (0, 0)),
                  pl.BlockSpec((1, kv_rank), lambda i: (0, 0)),
                  pl.BlockSpec((1, 128), lambda i: (0, 0))],
        out_specs=[pl.BlockSpec((tm, q_rank), lambda i: (i, 0)),
                   pl.BlockSpec((tm, kv_rank), lambda i: (i, 0)),
                   pl.BlockSpec((tm, 256), lambda i: (i, 0)),
                   pl.BlockSpec((tm, 128), lambda i: (i, 0)),
                   pl.BlockSpec((tm, 128), lambda i: (i, 0))],
        out_shape=[jax.ShapeDtypeStruct((t, q_rank), BF16),
                   jax.ShapeDtypeStruct((t, kv_rank), BF16),
                   jax.ShapeDtypeStruct((t, 256), BF16),
                   jax.ShapeDtypeStruct((t, 128), F32),
                   jax.ShapeDtypeStruct((t, 128), F32)],
        compiler_params=_params(("arbitrary",)),
        name="mla_prep",
    )(lat, positions.reshape(t, 1), q_norm.reshape(1, -1), kv_norm.reshape(1, -1), inv_freq128)


def _mla_body(qn_ref, qp_ref, kv_ref, kpe_ref, a_ref, wt_ref, o_ref, g_ref, kf_ref, vt_ref, *, scale):
    qi = pl.program_id(2)

    @pl.when(qi == 0)
    def _():
        for hh in range(2):
            kf_ref[hh, :, 0:128] = kv_ref[:, 256 * hh:256 * hh + 128]
            kf_ref[hh, :, 128:256] = kpe_ref[:, 128 * hh:128 * hh + 128]
            v = kv_ref[:, 256 * hh + 128:256 * hh + 256]
            vt_ref[hh] = v.astype(F32).T.astype(BF16)

    lane = lax.broadcasted_iota(I32, qp_ref.shape, 1)
    scores = []
    for hh in range(2):
        qp = qp_ref[...]
        keep = (lane >= MLA_ROPE_DIM) if hh else (lane < MLA_ROPE_DIM)
        qp = jnp.where(keep, qp, jnp.zeros_like(qp))
        qf = jnp.concatenate([qn_ref[:, 128 * hh:128 * hh + 128], qp], axis=1)
        scores.append(lax.dot_general(kf_ref[hh], qf, _NT, preferred_element_type=F32))

    gh = g_ref.shape[1] // 2
    for hh in range(2):
        g_ref[:, gh * hh:gh * (hh + 1)] = jax.nn.sigmoid(
            lax.dot_general(a_ref[...], wt_ref[gh * hh:gh * (hh + 1), :].astype(BF16), _NT,
                            preferred_element_type=F32)).astype(g_ref.dtype)
        st = scores[hh]
        m = jnp.max(st, axis=0, keepdims=True)
        e = jnp.exp2((st - m) * (scale * LOG2_E))
        l = jnp.sum(e, axis=0, keepdims=True)
        ot = lax.dot_general(vt_ref[hh], e.astype(BF16), _NN, preferred_element_type=F32)
        o_ref[:, 128 * hh:128 * hh + 128] = (ot / l).T.astype(o_ref.dtype)


def _mla_attention_and_gates(q2, kv, kpe, hn, w_t, *, gate_row0, n_gate, batch, seq, tq=512, gtn=512):
    t = q2.shape[0]
    k = hn.shape[1]
    nq = seq // tq
    hp = MLA_HEADS // 2
    nope_w = MLA_HEADS * MLA_NOPE_DIM
    scale = float((MLA_NOPE_DIM + MLA_ROPE_DIM) ** -0.5)
    n_steps = batch * hp * nq
    gcols = n_gate // gtn
    gtm = t * gcols // n_steps

    def step(b, h, q):
        return (b * hp + h) * nq + q

    return pl.pallas_call(
        functools.partial(_mla_body, scale=scale),
        grid=(batch, hp, nq),
        in_specs=[pl.BlockSpec((tq, 256), lambda b, h, q: (b * nq + q, h)),
                  pl.BlockSpec((tq, 128), lambda b, h, q: (b * nq + q, nope_w // 128 + h)),
                  pl.BlockSpec((seq, 512), lambda b, h, q: (b, h)),
                  pl.BlockSpec((seq, 256), lambda b, h, q: (b, 0)),
                  pl.BlockSpec((gtm, k), lambda b, h, q: (step(b, h, q) // gcols, 0),
                               pipeline_mode=pl.Buffered(1)),
                  pl.BlockSpec((pl.Element(gtn), pl.Element(k)),
                               lambda b, h, q: (pl.multiple_of(gate_row0 + (step(b, h, q) % gcols) * gtn, 8), 0))],
        out_specs=[pl.BlockSpec((tq, 256), lambda b, h, q: (b * nq + q, h)),
                   pl.BlockSpec((gtm, gtn), lambda b, h, q: (step(b, h, q) // gcols, step(b, h, q) % gcols))],
        out_shape=[jax.ShapeDtypeStruct((t, MLA_HEADS * MLA_V_DIM), BF16),
                   jax.ShapeDtypeStruct((t, n_gate), BF16)],
        scratch_shapes=[pltpu.VMEM((2, seq, 256), BF16), pltpu.VMEM((2, 128, seq), BF16)],
        compiler_params=_params(("arbitrary", "arbitrary", "arbitrary")),
        name="mla_attention_gates",
    )(q2, q2, kv, kpe, hn, w_t)


NA_Q_ROWS = 4
NA_K_ROWS = 12
NA_PAD_BLOCKS = NA_Q_ROWS
NA_BIAS_BLOCKS = 2 * NA_WIN_ROWS - 1 + 2 * NA_PAD_BLOCKS + 1


def _na_group(g, rows):
    r0 = NA_Q_ROWS * g
    w0 = min(max(r0 - NA_WIN_ROWS // 2, 0), rows - NA_K_ROWS)
    return r0, w0


def _na_bias_tables(rpb):
    n_heads = rpb.shape[0]
    kj = np.arange(GRID_W)[:, None]
    c = np.arange(GRID_W)[None, :]
    cs = np.clip(c - NA_WIN_COLS // 2, 0, GRID_W - NA_WIN_COLS)
    col_ok = (kj >= cs) & (kj < cs + NA_WIN_COLS)
    dc = kj - c + NA_WIN_COLS - 1
    pick = np.stack([(dc == dd) & col_ok for dd in range(2 * NA_WIN_COLS - 1)]).astype(np.float32)
    toep = jnp.einsum("hrd,dkc->hrkc", rpb.astype(F32), pick, precision=lax.Precision.HIGHEST)
    toep = jnp.where(col_ok[None, None], toep, NEG_BIG)
    n_off = 2 * NA_WIN_ROWS - 1
    blocks = jnp.pad(toep[:, ::-1], ((0, 0), (NA_PAD_BLOCKS, NA_BIAS_BLOCKS - NA_PAD_BLOCKS - n_off), (0, 0), (0, 0)),
                     constant_values=NEG_BIG)
    even = blocks.transpose(0, 2, 1, 3).reshape(n_heads, GRID_W, NA_BIAS_BLOCKS * GRID_W)
    odd = jnp.pad(even[:, :, GRID_W:], ((0, 0), (0, 0), (0, GRID_W)), constant_values=NEG_BIG)
    return even, odd


def _na_group_bias(even_ref, odd_ref, g, rows):
    r0, w0 = _na_group(g, rows)
    nq = NA_Q_ROWS * GRID_W
    slabs = []
    for k in range(NA_K_ROWS):
        ki = w0 + k
        bad = []
        for q in range(NA_Q_ROWS):
            rs = min(max(r0 + q - NA_WIN_ROWS // 2, 0), rows - NA_WIN_ROWS)
            if not rs <= ki < rs + NA_WIN_ROWS:
                bad.append(q)
        if len(bad) == NA_Q_ROWS:
            slabs.append(jnp.full((GRID_W, nq), NEG_BIG, F32))
            continue
        i0 = NA_PAD_BLOCKS + (NA_WIN_ROWS - 1) - (ki - r0)
        ref, first = (even_ref, i0) if i0 % 2 == 0 else (odd_ref, i0 - 1)
        slab = ref[0, :, first * GRID_W:first * GRID_W + nq]
        if bad:
            q_of_lane = lax.broadcasted_iota(I32, slab.shape, 1) // GRID_W
            outside = q_of_lane == bad[0]
            for q in bad[1:]:
                outside = outside | (q_of_lane == q)
            slab = jnp.where(outside, NEG_BIG, slab)
        slabs.append(slab)
    return jnp.concatenate(slabs, axis=0)


def _na_body(q_ref, k_ref, v_ref, even_ref, odd_ref, o_ref, *, rows, scale):
    vt = v_ref[...].astype(F32).T.astype(BF16)
    nq, nk = NA_Q_ROWS * GRID_W, NA_K_ROWS * GRID_W
    for g in range(rows // NA_Q_ROWS):
        r0, w0 = _na_group(g, rows)
        kwin = k_ref[w0 * GRID_W:w0 * GRID_W + nk, :]
        qg = q_ref[r0 * GRID_W:r0 * GRID_W + nq, :]
        st = (lax.dot_general(kwin, qg, _NT, preferred_element_type=F32) * scale
              + _na_group_bias(even_ref, odd_ref, g, rows))
        m = jnp.max(st, axis=0, keepdims=True)
        e = jnp.exp(st - m)
        l = jnp.sum(e, axis=0, keepdims=True)
        ot = lax.dot_general(vt[:, w0 * GRID_W:w0 * GRID_W + nk], e.astype(BF16), _NN,
                             preferred_element_type=F32)
        o_ref[r0 * GRID_W:r0 * GRID_W + nq, :] = (ot / l).T.astype(o_ref.dtype)


def _na_attention(qkv, bias_even, bias_odd, *, batch, seq):
    t = qkv.shape[0]
    rows = seq // GRID_W
    scale = float(NA_HEAD_DIM ** -0.5)
    bias_spec = pl.BlockSpec((1, GRID_W, NA_BIAS_BLOCKS * GRID_W), lambda h, b: (h, 0, 0))
    return pl.pallas_call(
        functools.partial(_na_body, rows=rows, scale=scale),
        grid=(NA_HEADS, batch),
        in_specs=[pl.BlockSpec((seq, NA_HEAD_DIM), lambda h, b: (b, h)),
                  pl.BlockSpec((seq, NA_HEAD_DIM), lambda h, b: (b, NA_HEADS + h)),
                  pl.BlockSpec((seq, NA_HEAD_DIM), lambda h, b: (b, 2 * NA_HEADS + h)),
                  bias_spec, bias_spec],
        out_specs=pl.BlockSpec((seq, NA_HEAD_DIM), lambda h, b: (b, h)),
        out_shape=jax.ShapeDtypeStruct((t, NA_HEADS * NA_HEAD_DIM), BF16),
        compiler_params=_params(("arbitrary", "arbitrary")),
        name="na_attention",
    )(qkv, qkv, qkv, bias_even, bias_odd)


def _router_body(x_ref, g_ref, wr_ref, br_ref, hn_ref, idx_ref, wts_ref, rank_ref, cnt_ref, carry_ref,
                 *, n_exp, tr):
    i = pl.program_id(0)

    @pl.when(i == 0)
    def _():
        carry_ref[...] = jnp.zeros_like(carry_ref)

    y = _rms(x_ref[...], g_ref[...])
    yb = y.astype(BF16)
    hn_ref[...] = _pack_bf16_pairs(y)

    logits = lax.dot_general(wr_ref[...].astype(BF16), yb, _NT, preferred_element_type=F32) + br_ref[...]
    eid = lax.broadcasted_iota(I32, (n_exp, tr), 0).astype(F32)
    work = logits
    vals, sels = [], []
    for k in range(TOP_K):
        m = jnp.max(work, axis=0, keepdims=True)
        first = jnp.min(jnp.where(work == m, eid, float(n_exp)), axis=0, keepdims=True)
        sel = eid == first
        vals.append(m)
        sels.append(sel)
        idx_ref[k:k + 1, :] = first.astype(I32)
        work = jnp.where(sel, -jnp.inf, work)
    es = [jnp.exp(v - vals[0]) for v in vals]
    denom = es[0] + es[1] + es[2] + es[3]
    for k in range(TOP_K):
        wts_ref[k:k + 1, :] = es[k] / denom

    chosen = jnp.zeros((n_exp, tr), F32)
    for sel in sels:
        chosen = chosen + sel.astype(F32)
    before = (lax.broadcasted_iota(I32, (tr, tr), 0) < lax.broadcasted_iota(I32, (tr, tr), 1)).astype(BF16)
    carry = carry_ref[:, 0:1]
    base = lax.dot_general(chosen.astype(BF16), before, _NN, preferred_element_type=F32) + carry
    for k in range(TOP_K):
        rank_ref[k:k + 1, :] = jnp.sum(jnp.where(sels[k], base, 0.0), axis=0, keepdims=True).astype(I32)
    total = carry + jnp.sum(chosen, axis=1, keepdims=True)
    carry_ref[...] = jnp.broadcast_to(total, carry_ref.shape)
    cnt_ref[...] = jnp.broadcast_to(total, cnt_ref.shape).astype(I32)


def _router(x1, g, wr_t, br, *, tr=512):
    t, d = x1.shape
    n_exp = wr_t.shape[0]
    return pl.pallas_call(
        functools.partial(_router_body, n_exp=n_exp, tr=tr),
        grid=(t // tr,),
        in_specs=[pl.BlockSpec((tr, d), lambda i: (i, 0)),
                  pl.BlockSpec((1, d), lambda i: (0, 0)),
                  pl.BlockSpec((n_exp, d), lambda i: (0, 0)),
                  pl.BlockSpec((n_exp, 1), lambda i: (0, 0))],
        out_specs=[pl.BlockSpec((tr, d // 2), lambda i: (i, 0)),
                   pl.BlockSpec((TOP_K, tr), lambda i: (0, i)),
                   pl.BlockSpec((TOP_K, tr), lambda i: (0, i)),
                   pl.BlockSpec((TOP_K, tr), lambda i: (0, i)),
                   pl.BlockSpec((n_exp, 128), lambda i: (0, 0))],
        out_shape=[jax.ShapeDtypeStruct((t, d // 2), U32),
                   jax.ShapeDtypeStruct((TOP_K, t), I32),
                   jax.ShapeDtypeStruct((TOP_K, t), F32),
                   jax.ShapeDtypeStruct((TOP_K, t), I32),
                   jax.ShapeDtypeStruct((n_exp, 128), I32)],
        scratch_shapes=[pltpu.VMEM((n_exp, 128), F32)],
        compiler_params=_params(("arbitrary",)),
        name="router_topk",
    )(x1, g.reshape(1, d), wr_t, br.reshape(n_exp, 1))


def _dispatch_body(pos_ref, hn_ref, xg_ref, sem, *, td):
    def row_copy(t, k):
        return pltpu.make_async_copy(hn_ref.at[pl.ds(t, 1), :],
                                     xg_ref.at[pl.ds(pos_ref[k, t], 1), :], sem)

    def start(t, c):
        for k in range(TOP_K):
            row_copy(t, k).start(priority=k % 2)
        return c

    def wait(t, c):
        for k in range(TOP_K):
            row_copy(t, k).wait()
        return c

    lax.fori_loop(0, td, start, 0, unroll=DMA_UNROLL)
    lax.fori_loop(0, td, wait, 0, unroll=DMA_UNROLL)


def _dispatch(pos, hn, n_rows, *, td=512):
    t = hn.shape[0]
    return pl.pallas_call(
        functools.partial(_dispatch_body, td=td),
        grid=(t // td,),
        in_specs=[pl.BlockSpec((TOP_K, td), lambda i: (0, i), memory_space=pltpu.SMEM),
                  pl.BlockSpec((td, hn.shape[1]), lambda i: (i, 0))],
        out_specs=pl.BlockSpec(memory_space=pl.ANY),
        out_shape=jax.ShapeDtypeStruct((n_rows, hn.shape[1]), hn.dtype),
        scratch_shapes=[pltpu.SemaphoreType.DMA(())],
        compiler_params=_params(("arbitrary",)),
        name="moe_dispatch",
    )(pos, hn)


def _expert_up_body(ie_ref, ib_ref, ins_ref, inr_ref, x_ref, w_ref, b_ref, o_ref, *, tn):
    w = pl.program_id(0)
    nsub = ins_ref[w]
    nrows = inr_ref[w]
    half = w_ref.shape[0] // 2

    for n in range(1, ITEM_SUBS + 1):
        @pl.when(nsub == n)
        def _(n=n):
            m = n * SUB
            row = lax.broadcasted_iota(I32, (m, 1), 0)
            xw = jnp.where(row < nrows, x_ref[0:m, :], jnp.uint32(0))
            lo, hi = _unpack_bf16_pairs(xw)
            lo, hi = lo.astype(BF16), hi.astype(BF16)
            hw = tn // 2

            def gate_up(c0):
                return (lax.dot_general(lo, w_ref[0:half, c0:c0 + hw].astype(BF16), _NN,
                                        preferred_element_type=F32)
                        + lax.dot_general(hi, w_ref[half:, c0:c0 + hw].astype(BF16), _NN,
                                          preferred_element_type=F32)
                        + b_ref[:, c0:c0 + hw])

            def swiglu(gate, up):
                gate = jnp.minimum(gate, SWIGLU_LIMIT)
                up = jnp.clip(up, -SWIGLU_LIMIT, SWIGLU_LIMIT)
                return (up + 1.0) * gate * jax.nn.sigmoid(SWIGLU_ALPHA * gate)

            first = gate_up(0)
            act_first = swiglu(first, pltpu.roll(first, hw - 1, 1))
            second = gate_up(hw)
            act_second = swiglu(pltpu.roll(second, 1, 1), second)
            lane = lax.broadcasted_iota(I32, (m, hw), 1)
            o_ref[0:m, :] = jnp.where((lane & 1) == 0, act_first, act_second).astype(o_ref.dtype)
            if m < ITEM_ROWS:
                o_ref[m:, :] = jnp.zeros((ITEM_ROWS - m, tn // 2), o_ref.dtype)


def _expert_up(items, n_items, xg, w_gu, b_gu, *, tn):
    item_e, item_blk, item_nsub, item_rows = items
    n_exp, d, f2 = w_gu.shape
    nj = f2 // tn
    n_rows = xg.shape[0]

    def jmap(j, ns, w):
        return jnp.where(ns[w] > 0, j, nj - 1)

    grid_spec = pltpu.PrefetchScalarGridSpec(
        num_scalar_prefetch=4,
        grid=(n_items, nj),
        in_specs=[pl.BlockSpec((ITEM_ROWS, xg.shape[1]), lambda w, j, ie, ib, ns, nr: (ib[w], 0)),
                  pl.BlockSpec((None, d, tn), lambda w, j, ie, ib, ns, nr: (ie[w], 0, jmap(j, ns, w))),
                  pl.BlockSpec((None, 1, tn), lambda w, j, ie, ib, ns, nr: (ie[w], 0, jmap(j, ns, w)))],
        out_specs=pl.BlockSpec((ITEM_ROWS, tn // 2), lambda w, j, ie, ib, ns, nr: (ib[w], jmap(j, ns, w))),
    )
    return pl.pallas_call(
        functools.partial(_expert_up_body, tn=tn),
        grid_spec=grid_spec,
        out_shape=jax.ShapeDtypeStruct((n_rows, f2 // 2), BF16),
        compiler_params=_params(("arbitrary", "arbitrary")),
        name="expert_gate_up",
    )(item_e, item_blk, item_nsub, item_rows, xg, w_gu, b_gu.reshape(n_exp, 1, f2))


def _interleave_rows_bf16(a, b):
    def rounded(x):
        bits = lax.bitcast_convert_type(x, U32)
        return bits + jnp.uint32(0x7FFF) + ((bits >> 16) & jnp.uint32(1))
    word = (rounded(b) & jnp.uint32(0xFFFF0000)) | (rounded(a) >> 16)
    return pltpu.bitcast(word, BF16)


def _expert_down_body(ie_ref, ib_ref, ins_ref, a_ref, w_ref, b_ref, o_ref, *, group):
    w = pl.program_id(0)
    nsub = ins_ref[w]

    for n in range(1, ITEM_SUBS + 1):
        @pl.when(nsub == n)
        def _(n=n):
            m = n * SUB
            wb = jnp.concatenate(
                [_interleave_rows_bf16(w_ref[g0:g0 + group // 2, :], w_ref[g0 + group // 2:g0 + group, :])
                 for g0 in range(0, w_ref.shape[0], group)], axis=0)
            y = lax.dot_general(a_ref[0:m, :], wb, _NN, preferred_element_type=F32)
            o_ref[0:m, :] = _pack_bf16_pairs(y + b_ref[...])
            if m < ITEM_ROWS:
                o_ref[m:, :] = jnp.zeros((ITEM_ROWS - m, o_ref.shape[1]), o_ref.dtype)


def _expert_down(items, n_items, act, w_d, b_d, *, group, tn):
    item_e, item_blk, item_nsub, _ = items
    n_exp, f, d = w_d.shape
    nj = d // tn
    n_rows = act.shape[0]

    def jmap(j, ns, w):
        return jnp.where(ns[w] > 0, j, nj - 1)

    grid_spec = pltpu.PrefetchScalarGridSpec(
        num_scalar_prefetch=3,
        grid=(n_items, nj),
        in_specs=[pl.BlockSpec((ITEM_ROWS, f), lambda w, j, ie, ib, ns: (ib[w], 0)),
                  pl.BlockSpec((None, f, tn), lambda w, j, ie, ib, ns: (ie[w], 0, jmap(j, ns, w))),
                  pl.BlockSpec((None, 1, tn), lambda w, j, ie, ib, ns: (ie[w], 0, jmap(j, ns, w)))],
        out_specs=pl.BlockSpec((ITEM_ROWS, tn // 2), lambda w, j, ie, ib, ns: (ib[w], jmap(j, ns, w))),
    )
    return pl.pallas_call(
        functools.partial(_expert_down_body, group=group),
        grid_spec=grid_spec,
        out_shape=jax.ShapeDtypeStruct((n_rows, d // 2), U32),
        compiler_params=_params(("arbitrary", "arbitrary")),
        name="expert_down",
    )(item_e, item_blk, item_nsub, act, w_d, b_d.reshape(n_exp, 1, d))


def _combine_body(pos_ref, posn_ref, wts_ref, x_ref, g_ref, yg_ref, o_ref, buf_ref, sem, *, tc, pw):
    i = pl.program_id(0)
    n = pl.num_programs(0)
    slot = lax.rem(i, 2)

    def row_copy(p_ref, s, t, k):
        return pltpu.make_async_copy(yg_ref.at[pl.ds(p_ref[k, t], 1), :],
                                     buf_ref.at[s, k, pl.ds(t, 1), :], sem.at[s])

    def start_tile(p_ref, s):
        def body(t, c):
            for k in range(TOP_K):
                row_copy(p_ref, s, t, k).start(priority=k % 2)
            return c
        lax.fori_loop(0, tc, body, 0, unroll=DMA_UNROLL)

    @pl.when(i == 0)
    def _():
        start_tile(pos_ref, 0)

    @pl.when(i + 1 < n)
    def _():
        start_tile(posn_ref, 1 - slot)

    def wait_body(t, c):
        for k in range(TOP_K):
            row_copy(pos_ref, slot, t, k).wait()
        return c
    lax.fori_loop(0, tc, wait_body, 0, unroll=DMA_UNROLL)

    parts = []
    for j in range(buf_ref.shape[3] // pw):
        lo = hi = None
        for k in range(TOP_K):
            wk = wts_ref[:, k:k + 1]
            l, h = _unpack_bf16_pairs(buf_ref[slot, k, :, j * pw:(j + 1) * pw])
            lo = wk * l if lo is None else lo + wk * l
            hi = wk * h if hi is None else hi + wk * h
        parts += [lo, hi]
    o_ref[...] = _rms(x_ref[...] + jnp.concatenate(parts, axis=1), g_ref[...])


def _combine(pos, wts_t, x1, g, yg, *, pw, tc=128):
    t, d = x1.shape
    nt = t // tc
    return pl.pallas_call(
        functools.partial(_combine_body, tc=tc, pw=pw),
        grid=(nt,),
        in_specs=[pl.BlockSpec((TOP_K, tc), lambda i: (0, i), memory_space=pltpu.SMEM),
                  pl.BlockSpec((TOP_K, tc), lambda i: (0, jnp.minimum(i + 1, nt - 1)), memory_space=pltpu.SMEM),
                  pl.BlockSpec((tc, TOP_K), lambda i: (i, 0)),
                  pl.BlockSpec((tc, d), lambda i: (i, 0)),
                  pl.BlockSpec((1, d), lambda i: (0, 0)),
                  pl.BlockSpec(memory_space=pl.ANY)],
        out_specs=pl.BlockSpec((tc, d), lambda i: (i, 0)),
        out_shape=jax.ShapeDtypeStruct((t, d), F32),
        scratch_shapes=[pltpu.VMEM((2, TOP_K, tc, d // 2), U32), pltpu.SemaphoreType.DMA((2,))],
        compiler_params=_params(("arbitrary",)),
        name="moe_combine_norm",
    )(pos, pos, wts_t, x1, g.reshape(1, d), yg)


def _plan_items(counts, n_assign):
    n_exp = counts.shape[0]
    max_items = n_assign // ITEM_ROWS + n_exp
    nsub_e = (counts + SUB - 1) // SUB
    nitem_e = (counts + ITEM_ROWS - 1) // ITEM_ROWS
    last_item_e = jnp.cumsum(nitem_e)
    first_item_e = last_item_e - nitem_e
    total = last_item_e[-1]
    w = jnp.arange(max_items, dtype=I32)
    valid = w < total
    e_w = jnp.minimum(jnp.searchsorted(last_item_e, w, side="right"), n_exp - 1).astype(I32)
    e_last = e_w[jnp.maximum(total - 1, 0)]
    e_w = jnp.where(valid, e_w, e_last)
    c_w = w - first_item_e[e_w]
    nsub_w = jnp.where(valid, jnp.clip(nsub_e[e_w] - ITEM_SUBS * c_w, 0, ITEM_SUBS), 0).astype(I32)
    rows_w = jnp.where(valid, jnp.clip(counts[e_w] - ITEM_ROWS * c_w, 0, ITEM_ROWS), 0).astype(I32)
    blk_w = jnp.where(valid, w, max_items).astype(I32)
    row_off_e = (first_item_e * ITEM_ROWS).astype(I32)
    return (e_w, blk_w, nsub_w, rows_w), total.astype(I32), row_off_e, (max_items + 1) * ITEM_ROWS


def kernel(x, positions, norm_mix, w_in, q_a_norm, w_q_b, kv_a_norm, w_kv_b, na_rpb, w_proj_a, w_proj_b, w_out, norm_ffn, w_router, b_router, w_gate_up, b_gate_up, w_down, b_down, norm_final):
    b, s, d = x.shape
    t = b * s
    na_w = NA_HEADS * NA_HEAD_DIM
    q_rank, kv_rank = q_a_norm.shape[1], kv_a_norm.shape[1]
    lat0 = 3 * na_w
    gate0 = lat0 + q_rank + kv_rank + MLA_ROPE_DIM
    xf = x.reshape(t, d)
    w_in_t = jnp.swapaxes(w_in, 1, 2)[0]

    hn = _rmsnorm(xf, norm_mix[0])
    qkv = _matmul_t(hn, w_in_t, row0=0, ncols=lat0, tm=2048, tn=512, out_dtype=BF16, name="proj_qkv")
    lat = _matmul_t(hn, w_in_t, row0=lat0, ncols=2048, tm=2048, tn=512, out_dtype=F32, name="proj_latent")
    oa = _na_attention(qkv, *_na_bias_tables(na_rpb[0]), batch=b, seq=s)

    half = MLA_ROPE_DIM // 2
    inv_freq = ROPE_THETA ** (-(jnp.arange(half, dtype=F32) * 2.0) / MLA_ROPE_DIM)
    inv_freq128 = jnp.tile(inv_freq, 128 // half).reshape(1, 128)
    cqn, ckvn, kpe, cos, sin = _mla_prep(lat, positions.reshape(t), q_a_norm[0], kv_a_norm[0], inv_freq128)
    qk_dim = MLA_NOPE_DIM + MLA_ROPE_DIM
    wq = w_q_b[0].reshape(q_rank, MLA_HEADS, qk_dim)
    wq = jnp.concatenate([wq[:, :, :MLA_NOPE_DIM].reshape(q_rank, -1),
                          wq[:, :, MLA_NOPE_DIM:].reshape(q_rank, -1)], axis=1)
    q2 = _qproj(cqn, wq, cos, sin, rope_col0=MLA_HEADS * MLA_NOPE_DIM, tm=2048, tn=1024)
    kv = _matmul(ckvn, w_kv_b[0], col0=0, ncols=w_kv_b.shape[2], tm=2048, tn=1024, out_dtype=BF16,
                 name="kv_proj")
    ob, gates = _mla_attention_and_gates(q2, kv, kpe, hn, w_in_t, gate_row0=gate0, n_gate=2 * d,
                                         batch=b, seq=s)

    y = _merge(oa, ob, w_proj_a[0], w_proj_b[0], gates, tm=2048, tn=512)
    x1 = _outproj(y, w_out[0], xf, tm=2048, tn=512)

    hn_packed, idx, wts, rank, counts = _router(x1, norm_ffn[0], w_router[0].T, b_router[0])
    items, n_items, row_off, n_rows = _plan_items(counts[:, 0], t * TOP_K)
    onehot = idx[None] == jnp.arange(row_off.shape[0], dtype=I32)[:, None, None]
    pos = jnp.sum(jnp.where(onehot, row_off[:, None, None], 0), axis=0) + rank
    xg = _dispatch(pos, hn_packed, n_rows)
    up_tn, down_tn = 512, 1024
    act = _expert_up(items, n_items, xg, w_gate_up[0], b_gate_up[0], tn=up_tn)
    yg = _expert_down(items, n_items, act, w_down[0], b_down[0], group=up_tn // 2, tn=down_tn)
    out = _combine(pos, wts.T, x1, norm_final, yg, pw=down_tn // 2)
    return out.reshape(b, s, d)
```

```python
import functools

import numpy as np
import jax
import jax.numpy as jnp
from jax import lax
from jax.experimental import pallas as pl
from jax.experimental.pallas import tpu as pltpu

F32 = jnp.float32
BF16 = jnp.bfloat16
U32 = jnp.uint32
I32 = jnp.int32

GRID_W = 64
NA_HEADS = 16
NA_HEAD_DIM = 128
NA_WIN_ROWS = 8
NA_WIN_COLS = 16
MLA_HEADS = 16
MLA_NOPE_DIM = 128
MLA_ROPE_DIM = 64
MLA_V_DIM = 128
ROPE_THETA = 10000.0
TOP_K = 4
SWIGLU_LIMIT = 7.0
SWIGLU_ALPHA = 1.702
NORM_EPS = 1e-6

NEG_BIG = -1e30
LOG2_E = 1.4426950408889634

V7X_VMEM_BYTES = 64 * 1024 * 1024
VMEM_LIMIT = V7X_VMEM_BYTES - 4 * 1024 * 1024

SUB = 128
ITEM_SUBS = 10
ITEM_ROWS = SUB * ITEM_SUBS
DMA_UNROLL = 8

_NN = (((1,), (0,)), ((), ()))
_NT = (((1,), (1,)), ((), ()))


def _params(sem=None):
    return pltpu.CompilerParams(vmem_limit_bytes=VMEM_LIMIT, dimension_semantics=sem)


def _rms(x, g):
    return x * lax.rsqrt(jnp.mean(x * x, axis=-1, keepdims=True) + NORM_EPS) * g


def _pack_bf16_pairs(x):
    n = x.shape[1] // 2
    lo = lax.bitcast_convert_type(x[:, :n].astype(BF16).astype(F32), U32)
    hi = lax.bitcast_convert_type(x[:, n:].astype(BF16).astype(F32), U32)
    return (hi & jnp.uint32(0xFFFF0000)) | (lo >> 16)


def _unpack_bf16_pairs(w):
    return (lax.bitcast_convert_type(w << 16, F32),
            lax.bitcast_convert_type(w & jnp.uint32(0xFFFF0000), F32))


def _rmsnorm_body(x_ref, g_ref, o_ref):
    o_ref[...] = _rms(x_ref[...], g_ref[...]).astype(o_ref.dtype)


def _rmsnorm(x, g, *, tm=512):
    t, d = x.shape
    return pl.pallas_call(
        _rmsnorm_body,
        grid=(t // tm,),
        in_specs=[pl.BlockSpec((tm, d), lambda i: (i, 0)),
                  pl.BlockSpec((1, d), lambda i: (0, 0))],
        out_specs=pl.BlockSpec((tm, d), lambda i: (i, 0)),
        out_shape=jax.ShapeDtypeStruct((t, d), BF16),
        compiler_params=_params(("arbitrary",)),
        name="rmsnorm",
    )(x, g.reshape(1, d))


def _mm_body(a_ref, w_ref, o_ref):
    acc = lax.dot_general(a_ref[...], w_ref[...].astype(BF16), _NN, preferred_element_type=F32)
    o_ref[...] = acc.astype(o_ref.dtype)


def _matmul(a, w, *, col0, ncols, tm, tn, out_dtype, name):
    t, k = a.shape
    return pl.pallas_call(
        _mm_body,
        grid=(t // tm, ncols // tn),
        in_specs=[pl.BlockSpec((tm, k), lambda i, j: (i, 0), pipeline_mode=pl.Buffered(1)),
                  pl.BlockSpec((k, tn), lambda i, j: (0, j + col0 // tn))],
        out_specs=pl.BlockSpec((tm, tn), lambda i, j: (i, j)),
        out_shape=jax.ShapeDtypeStruct((t, ncols), out_dtype),
        compiler_params=_params(("arbitrary", "arbitrary")),
        name=name,
    )(a, w)


def _mm_t_body(a_ref, wt_ref, o_ref):
    acc = lax.dot_general(a_ref[...], wt_ref[...].astype(BF16), _NT, preferred_element_type=F32)
    o_ref[...] = acc.astype(o_ref.dtype)


def _matmul_t(a, w_t, *, row0, ncols, tm, tn, out_dtype, name):
    t, k = a.shape
    return pl.pallas_call(
        _mm_t_body,
        grid=(t // tm, ncols // tn),
        in_specs=[pl.BlockSpec((tm, k), lambda i, j: (i, 0), pipeline_mode=pl.Buffered(1)),
                  pl.BlockSpec((pl.Element(tn), pl.Element(k)),
                               lambda i, j: (pl.multiple_of(row0 + j * tn, 8), 0))],
        out_specs=pl.BlockSpec((tm, tn), lambda i, j: (i, j)),
        out_shape=jax.ShapeDtypeStruct((t, ncols), out_dtype),
        compiler_params=_params(("arbitrary", "arbitrary")),
        name=name,
    )(a, w_t)


def _qproj_body(a_ref, w_ref, cos_ref, sin_ref, o_ref, *, first_rope_block, tn):
    acc = lax.dot_general(a_ref[...], w_ref[...].astype(BF16), _NN, preferred_element_type=F32)
    j = pl.program_id(1)

    @pl.when(j < first_rope_block)
    def _():
        o_ref[...] = acc.astype(o_ref.dtype)

    @pl.when(j >= first_rope_block)
    def _():
        reps = tn // 128
        c = jnp.concatenate([cos_ref[...]] * reps, axis=1)
        s = jnp.concatenate([sin_ref[...]] * reps, axis=1)
        lane = lax.broadcasted_iota(I32, acc.shape, 1)
        first = (lane & (MLA_ROPE_DIM - 1)) < MLA_ROPE_DIM // 2
        half = MLA_ROPE_DIM // 2
        partner = jnp.where(first, pltpu.roll(acc, tn - half, 1), pltpu.roll(acc, half, 1))
        o_ref[...] = (acc * c + partner * s).astype(o_ref.dtype)


def _qproj(a, w, cos, sin, *, rope_col0, tm, tn):
    t, k = a.shape
    n = w.shape[1]
    return pl.pallas_call(
        functools.partial(_qproj_body, first_rope_block=rope_col0 // tn, tn=tn),
        grid=(t // tm, n // tn),
        in_specs=[pl.BlockSpec((tm, k), lambda i, j: (i, 0), pipeline_mode=pl.Buffered(1)),
                  pl.BlockSpec((k, tn), lambda i, j: (0, j)),
                  pl.BlockSpec((tm, 128), lambda i, j: (i, 0)),
                  pl.BlockSpec((tm, 128), lambda i, j: (i, 0))],
        out_specs=pl.BlockSpec((tm, tn), lambda i, j: (i, j)),
        out_shape=jax.ShapeDtypeStruct((t, n), BF16),
        compiler_params=_params(("arbitrary", "arbitrary")),
        name="q_proj_rope",
    )(a, w, cos, sin)


def _merge_body(oa_ref, ob_ref, wa_ref, wb_ref, ga_ref, gb_ref, o_ref):
    pa = lax.dot_general(oa_ref[...], wa_ref[...].astype(BF16), _NN, preferred_element_type=F32)
    pb = lax.dot_general(ob_ref[...], wb_ref[...].astype(BF16), _NN, preferred_element_type=F32)
    o_ref[...] = (ga_ref[...] * pa + gb_ref[...] * pb).astype(o_ref.dtype)


def _merge(oa, ob, wa, wb, gates, *, tm, tn):
    t, k = oa.shape
    n = wa.shape[1]
    nb = n // tn
    return pl.pallas_call(
        _merge_body,
        grid=(t // tm, nb),
        in_specs=[pl.BlockSpec((tm, k), lambda i, j: (i, 0), pipeline_mode=pl.Buffered(1)),
                  pl.BlockSpec((tm, k), lambda i, j: (i, 0), pipeline_mode=pl.Buffered(1)),
                  pl.BlockSpec((k, tn), lambda i, j: (0, j)),
                  pl.BlockSpec((k, tn), lambda i, j: (0, j)),
                  pl.BlockSpec((tm, tn), lambda i, j: (i, j)),
                  pl.BlockSpec((tm, tn), lambda i, j: (i, j + nb))],
        out_specs=pl.BlockSpec((tm, tn), lambda i, j: (i, j)),
        out_shape=jax.ShapeDtypeStruct((t, n), BF16),
        compiler_params=_params(("arbitrary", "arbitrary")),
        name="gated_merge",
    )(oa, ob, wa, wb, gates, gates)


def _outproj_body(a_ref, w_ref, r_ref, o_ref):
    acc = lax.dot_general(a_ref[...], w_ref[...].astype(BF16), _NN, preferred_element_type=F32)
    o_ref[...] = r_ref[...] + acc


def _outproj(a, w, resid, *, tm, tn):
    t, k = a.shape
    n = w.shape[1]
    return pl.pallas_call(
        _outproj_body,
        grid=(t // tm, n // tn),
        in_specs=[pl.BlockSpec((tm, k), lambda i, j: (i, 0), pipeline_mode=pl.Buffered(1)),
                  pl.BlockSpec((k, tn), lambda i, j: (0, j)),
                  pl.BlockSpec((tm, tn), lambda i, j: (i, j))],
        out_specs=pl.BlockSpec((tm, tn), lambda i, j: (i, j)),
        out_shape=jax.ShapeDtypeStruct((t, n), F32),
        compiler_params=_params(("arbitrary", "arbitrary")),
        name="out_proj_residual",
    )(a, w, resid)


def _mla_prep_body(lat_ref, pos_ref, qn_ref, kvn_ref, invf_ref,
                   cq_ref, ckv_ref, kpe_ref, cos_ref, sin_ref, *, q_rank, kv_rank):
    cq_ref[...] = _rms(lat_ref[:, 0:q_rank], qn_ref[...]).astype(cq_ref.dtype)
    ckv_ref[...] = _rms(lat_ref[:, q_rank:q_rank + kv_rank], kvn_ref[...]).astype(ckv_ref.dtype)
    kr = lat_ref[:, q_rank + kv_rank:q_rank + kv_rank + 128]
    ang = pos_ref[...].astype(F32) * invf_ref[...]
    c = jnp.cos(ang)
    s = jnp.sin(ang)
    lane = lax.broadcasted_iota(I32, ang.shape, 1)
    half = MLA_ROPE_DIM // 2
    first = (lane & (MLA_ROPE_DIM - 1)) < half
    s = jnp.where(first, -s, s)
    partner = jnp.where(first, pltpu.roll(kr, 128 - half, 1), pltpu.roll(kr, half, 1))
    kpe = jnp.where(lane < MLA_ROPE_DIM, kr * c + partner * s, 0.0)
    kpe_ref[:, 0:128] = kpe.astype(kpe_ref.dtype)
    kpe_ref[:, 128:256] = pltpu.roll(kpe, MLA_ROPE_DIM, 1).astype(kpe_ref.dtype)
    cos_ref[...] = c
    sin_ref[...] = s


def _mla_prep(lat, positions, q_norm, kv_norm, inv_freq128, *, tm=512):
    t = lat.shape[0]
    q_rank, kv_rank = q_norm.shape[0], kv_norm.shape[0]
    return pl.pallas_call(
        functools.partial(_mla_prep_body, q_rank=q_rank, kv_rank=kv_rank),
        grid=(t // tm,),
        in_specs=[pl.BlockSpec((tm, lat.shape[1]), lambda i: (i, 0)),
                  pl.BlockSpec((tm, 1), lambda i: (i, 0)),
                  pl.BlockSpec((1, q_rank), lambda i: (0, 0)),
                  pl.BlockSpec((1, kv_rank), lambda i: (0, 0)),
                  pl.BlockSpec((1, 128), lambda i: (0, 0))],
        out_specs=[pl.BlockSpec((tm, q_rank), lambda i: (i, 0)),
                   pl.BlockSpec((tm, kv_rank), lambda i: (i, 0)),
                   pl.BlockSpec((tm, 256), lambda i: (i, 0)),
                   pl.BlockSpec((tm, 128), lambda i: (i, 0)),
                   pl.BlockSpec((tm, 128), lambda i: (i, 0))],
        out_shape=[jax.ShapeDtypeStruct((t, q_rank), BF16),
                   jax.ShapeDtypeStruct((t, kv_rank), BF16),
                   jax.ShapeDtypeStruct((t, 256), BF16),
                   jax.ShapeDtypeStruct((t, 128), F32),
                   jax.ShapeDtypeStruct((t, 128), F32)],
        compiler_params=_params(("arbitrary",)),
        name="mla_prep",
    )(lat, positions.reshape(t, 1), q_norm.reshape(1, -1), kv_norm.reshape(1, -1), inv_freq128)


NA_Q_ROWS = 4
NA_K_ROWS = 12
NA_PAD_BLOCKS = NA_Q_ROWS
NA_BIAS_BLOCKS = 2 * NA_WIN_ROWS - 1 + 2 * NA_PAD_BLOCKS + 1


def _na_group(g, rows):
    r0 = NA_Q_ROWS * g
    w0 = min(max(r0 - NA_WIN_ROWS // 2, 0), rows - NA_K_ROWS)
    return r0, w0


def _na_bias_tables(rpb):
    n_heads = rpb.shape[0]
    kj = np.arange(GRID_W)[:, None]
    c = np.arange(GRID_W)[None, :]
    cs = np.clip(c - NA_WIN_COLS // 2, 0, GRID_W - NA_WIN_COLS)
    col_ok = (kj >= cs) & (kj < cs + NA_WIN_COLS)
    dc = kj - c + NA_WIN_COLS - 1
    pick = np.stack([(dc == dd) & col_ok for dd in range(2 * NA_WIN_COLS - 1)]).astype(np.float32)
    toep = jnp.einsum("hrd,dkc->hrkc", rpb.astype(F32), pick, precision=lax.Precision.HIGHEST)
    toep = jnp.where(col_ok[None, None], toep, NEG_BIG)
    n_off = 2 * NA_WIN_ROWS - 1
    blocks = jnp.pad(toep[:, ::-1], ((0, 0), (NA_PAD_BLOCKS, NA_BIAS_BLOCKS - NA_PAD_BLOCKS - n_off), (0, 0), (0, 0)),
                     constant_values=NEG_BIG)
    even = blocks.transpose(0, 2, 1, 3).reshape(n_heads, GRID_W, NA_BIAS_BLOCKS * GRID_W)
    odd = jnp.pad(even[:, :, GRID_W:], ((0, 0), (0, 0), (0, GRID_W)), constant_values=NEG_BIG)
    return even, odd


def _na_group_bias(even_ref, odd_ref, g, rows):
    r0, w0 = _na_group(g, rows)
    nq = NA_Q_ROWS * GRID_W
    slabs = []
    for k in range(NA_K_ROWS):
        ki = w0 + k
        bad = []
        for q in range(NA_Q_ROWS):
            rs = min(max(r0 + q - NA_WIN_ROWS // 2, 0), rows - NA_WIN_ROWS)
            if not rs <= ki < rs + NA_WIN_ROWS:
                bad.append(q)
        if len(bad) == NA_Q_ROWS:
            slabs.append(jnp.full((GRID_W, nq), NEG_BIG, F32))
            continue
        i0 = NA_PAD_BLOCKS + (NA_WIN_ROWS - 1) - (ki - r0)
        ref, first = (even_ref, i0) if i0 % 2 == 0 else (odd_ref, i0 - 1)
        slab = ref[0, :, first * GRID_W:first * GRID_W + nq]
        if bad:
            q_of_lane = lax.broadcasted_iota(I32, slab.shape, 1) // GRID_W
            outside = q_of_lane == bad[0]
            for q in bad[1:]:
                outside = outside | (q_of_lane == q)
            slab = jnp.where(outside, NEG_BIG, slab)
        slabs.append(slab)
    return jnp.concatenate(slabs, axis=0)


def _attn_body(qn_ref, qp_ref, kv_ref, kpe_ref, a_ref, wt_ref, nq_ref, nk_ref, nv_ref, even_ref, odd_ref,
               ob_ref, g_ref, oa_ref, kf_ref, vt_ref, *, mla_scale, na_scale, rows, steps_per_na_head):
    qi = pl.program_id(2)

    @pl.when(qi == 0)
    def _():
        for hh in range(2):
            kf_ref[hh, :, 0:128] = kv_ref[:, 256 * hh:256 * hh + 128]
            kf_ref[hh, :, 128:256] = kpe_ref[:, 128 * hh:128 * hh + 128]
            v = kv_ref[:, 256 * hh + 128:256 * hh + 256]
            vt_ref[hh] = v.astype(F32).T.astype(BF16)

    n_groups = rows // NA_Q_ROWS
    per_step = n_groups // steps_per_na_head
    nqq, nkk = NA_Q_ROWS * GRID_W, NA_K_ROWS * GRID_W

    def na_groups(nvt, groups):
        for g in groups:
            r0, w0 = _na_group(g, rows)
            kwin = nk_ref[w0 * GRID_W:w0 * GRID_W + nkk, :]
            qg = nq_ref[r0 * GRID_W:r0 * GRID_W + nqq, :]
            st = (lax.dot_general(kwin, qg, _NT, preferred_element_type=F32) * na_scale
                  + _na_group_bias(even_ref, odd_ref, g, rows))
            m = jnp.max(st, axis=0, keepdims=True)
            e = jnp.exp(st - m)
            l = jnp.sum(e, axis=0, keepdims=True)
            ot = lax.dot_general(nvt[:, w0 * GRID_W:w0 * GRID_W + nkk], e.astype(BF16), _NN,
                                 preferred_element_type=F32)
            oa_ref[r0 * GRID_W:r0 * GRID_W + nqq, :] = (ot / l).T.astype(oa_ref.dtype)

    for part in range(steps_per_na_head):
        @pl.when(lax.rem(qi, steps_per_na_head) == part)
        def _(part=part):
            groups = list(range(part * per_step, (part + 1) * per_step))
            nvt = nv_ref[...].astype(F32).T.astype(BF16)
            lane = lax.broadcasted_iota(I32, qp_ref.shape, 1)
            scores = []
            for hh in range(2):
                qp = qp_ref[...]
                keep = (lane >= MLA_ROPE_DIM) if hh else (lane < MLA_ROPE_DIM)
                qp = jnp.where(keep, qp, jnp.zeros_like(qp))
                qf = jnp.concatenate([qn_ref[:, 128 * hh:128 * hh + 128], qp], axis=1)
                scores.append(lax.dot_general(kf_ref[hh], qf, _NT, preferred_element_type=F32))
            gh = g_ref.shape[1] // 2
            for hh in range(2):
                g_ref[:, gh * hh:gh * (hh + 1)] = jax.nn.sigmoid(
                    lax.dot_general(a_ref[...], wt_ref[gh * hh:gh * (hh + 1), :].astype(BF16), _NT,
                                    preferred_element_type=F32)).astype(g_ref.dtype)
                st = scores[hh]
                m = jnp.max(st, axis=0, keepdims=True)
                e = jnp.exp2((st - m) * (mla_scale * LOG2_E))
                l = jnp.sum(e, axis=0, keepdims=True)
                ot = lax.dot_general(vt_ref[hh], e.astype(BF16), _NN, preferred_element_type=F32)
                ob_ref[:, 128 * hh:128 * hh + 128] = (ot / l).T.astype(ob_ref.dtype)
                na_groups(nvt, groups[hh * (per_step // 2):(hh + 1) * (per_step // 2)])


def _fused_attention(q2, kv, kpe, hn, w_t, qkv, bias_even, bias_odd, *, gate_row0, n_gate, batch, seq,
                     tq=512, gtn=512):
    t = q2.shape[0]
    k = hn.shape[1]
    nq = seq // tq
    hp = MLA_HEADS // 2
    nope_w = MLA_HEADS * MLA_NOPE_DIM
    rows = seq // GRID_W
    n_steps = batch * hp * nq
    gcols = n_gate // gtn
    gtm = t * gcols // n_steps
    steps_per_na_head = nq * hp // NA_HEADS
    assert steps_per_na_head * NA_HEADS == nq * hp and (rows // NA_Q_ROWS) % (2 * steps_per_na_head) == 0

    def step(b, h, q):
        return (b * hp + h) * nq + q

    def na_head(h, q):
        return (h * nq + q) // steps_per_na_head

    bias_spec = pl.BlockSpec((1, GRID_W, NA_BIAS_BLOCKS * GRID_W), lambda b, h, q: (na_head(h, q), 0, 0))
    return pl.pallas_call(
        functools.partial(_attn_body, mla_scale=float((MLA_NOPE_DIM + MLA_ROPE_DIM) ** -0.5),
                          na_scale=float(NA_HEAD_DIM ** -0.5), rows=rows, steps_per_na_head=steps_per_na_head),
        grid=(batch, hp, nq),
        in_specs=[pl.BlockSpec((tq, 256), lambda b, h, q: (b * nq + q, h)),
                  pl.BlockSpec((tq, 128), lambda b, h, q: (b * nq + q, nope_w // 128 + h)),
                  pl.BlockSpec((seq, 512), lambda b, h, q: (b, h)),
                  pl.BlockSpec((seq, 256), lambda b, h, q: (b, 0)),
                  pl.BlockSpec((gtm, k), lambda b, h, q: (step(b, h, q) // gcols, 0),
                               pipeline_mode=pl.Buffered(1)),
                  pl.BlockSpec((pl.Element(gtn), pl.Element(k)),
                               lambda b, h, q: (pl.multiple_of(gate_row0 + (step(b, h, q) % gcols) * gtn, 8), 0)),
                  pl.BlockSpec((seq, NA_HEAD_DIM), lambda b, h, q: (b, na_head(h, q))),
                  pl.BlockSpec((seq, NA_HEAD_DIM), lambda b, h, q: (b, NA_HEADS + na_head(h, q))),
                  pl.BlockSpec((seq, NA_HEAD_DIM), lambda b, h, q: (b, 2 * NA_HEADS + na_head(h, q))),
                  bias_spec, bias_spec],
        out_specs=[pl.BlockSpec((tq, 256), lambda b, h, q: (b * nq + q, h)),
                   pl.BlockSpec((gtm, gtn), lambda b, h, q: (step(b, h, q) // gcols, step(b, h, q) % gcols)),
                   pl.BlockSpec((seq, NA_HEAD_DIM), lambda b, h, q: (b, na_head(h, q)))],
        out_shape=[jax.ShapeDtypeStruct((t, MLA_HEADS * MLA_V_DIM), BF16),
                   jax.ShapeDtypeStruct((t, n_gate), BF16),
                   jax.ShapeDtypeStruct((t, NA_HEADS * NA_HEAD_DIM), BF16)],
        scratch_shapes=[pltpu.VMEM((2, seq, 256), BF16), pltpu.VMEM((2, 128, seq), BF16)],
        compiler_params=_params(("arbitrary", "arbitrary", "arbitrary")),
        name="fused_attention_gates",
    )(q2, q2, kv, kpe, hn, w_t, qkv, qkv, qkv, bias_even, bias_odd)


def _router_body(x_ref, g_ref, wr_ref, br_ref, hn_ref, idx_ref, wts_ref, rank_ref, cnt_ref, carry_ref,
                 *, n_exp, tr):
    i = pl.program_id(0)

    @pl.when(i == 0)
    def _():
        carry_ref[...] = jnp.zeros_like(carry_ref)

    y = _rms(x_ref[...], g_ref[...])
    yb = y.astype(BF16)
    hn_ref[...] = _pack_bf16_pairs(y)

    logits = lax.dot_general(wr_ref[...].astype(BF16), yb, _NT, preferred_element_type=F32) + br_ref[...]
    eid = lax.broadcasted_iota(I32, (n_exp, tr), 0).astype(F32)
    work = logits
    vals, sels = [], []
    for k in range(TOP_K):
        m = jnp.max(work, axis=0, keepdims=True)
        first = jnp.min(jnp.where(work == m, eid, float(n_exp)), axis=0, keepdims=True)
        sel = eid == first
        vals.append(m)
        sels.append(sel)
        idx_ref[k:k + 1, :] = first.astype(I32)
        work = jnp.where(sel, -jnp.inf, work)
    es = [jnp.exp(v - vals[0]) for v in vals]
    denom = es[0] + es[1] + es[2] + es[3]
    for k in range(TOP_K):
        wts_ref[k:k + 1, :] = es[k] / denom

    chosen = jnp.zeros((n_exp, tr), F32)
    for sel in sels:
        chosen = chosen + sel.astype(F32)
    before = (lax.broadcasted_iota(I32, (tr, tr), 0) < lax.broadcasted_iota(I32, (tr, tr), 1)).astype(BF16)
    carry = carry_ref[:, 0:1]
    base = lax.dot_general(chosen.astype(BF16), before, _NN, preferred_element_type=F32) + carry
    for k in range(TOP_K):
        rank_ref[k:k + 1, :] = jnp.sum(jnp.where(sels[k], base, 0.0), axis=0, keepdims=True).astype(I32)
    total = carry + jnp.sum(chosen, axis=1, keepdims=True)
    carry_ref[...] = jnp.broadcast_to(total, carry_ref.shape)
    cnt_ref[...] = jnp.broadcast_to(total, cnt_ref.shape).astype(I32)


def _router(x1, g, wr_t, br, *, tr=512):
    t, d = x1.shape
    n_exp = wr_t.shape[0]
    return pl.pallas_call(
        functools.partial(_router_body, n_exp=n_exp, tr=tr),
        grid=(t // tr,),
        in_specs=[pl.BlockSpec((tr, d), lambda i: (i, 0)),
                  pl.BlockSpec((1, d), lambda i: (0, 0)),
                  pl.BlockSpec((n_exp, d), lambda i: (0, 0)),
                  pl.BlockSpec((n_exp, 1), lambda i: (0, 0))],
        out_specs=[pl.BlockSpec((tr, d // 2), lambda i: (i, 0)),
                   pl.BlockSpec((TOP_K, tr), lambda i: (0, i)),
                   pl.BlockSpec((TOP_K, tr), lambda i: (0, i)),
                   pl.BlockSpec((TOP_K, tr), lambda i: (0, i)),
                   pl.BlockSpec((n_exp, 128), lambda i: (0, 0))],
        out_shape=[jax.ShapeDtypeStruct((t, d // 2), U32),
                   jax.ShapeDtypeStruct((TOP_K, t), I32),
                   jax.ShapeDtypeStruct((TOP_K, t), F32),
                   jax.ShapeDtypeStruct((TOP_K, t), I32),
                   jax.ShapeDtypeStruct((n_exp, 128), I32)],
        scratch_shapes=[pltpu.VMEM((n_exp, 128), F32)],
        compiler_params=_params(("arbitrary",)),
        name="router_topk",
    )(x1, g.reshape(1, d), wr_t, br.reshape(n_exp, 1))


def _dispatch_body(pos_ref, hn_ref, xg_ref, sem, *, td):
    def row_copy(t, k):
        return pltpu.make_async_copy(hn_ref.at[pl.ds(t, 1), :],
                                     xg_ref.at[pl.ds(pos_ref[k, t], 1), :], sem)

    def start(t, c):
        for k in range(TOP_K):
            row_copy(t, k).start(priority=k % 2)
        return c

    def wait(t, c):
        for k in range(TOP_K):
            row_copy(t, k).wait()
        return c

    lax.fori_loop(0, td, start, 0, unroll=DMA_UNROLL)
    lax.fori_loop(0, td, wait, 0, unroll=DMA_UNROLL)


def _dispatch(pos, hn, n_rows, *, td=512):
    t = hn.shape[0]
    return pl.pallas_call(
        functools.partial(_dispatch_body, td=td),
        grid=(t // td,),
        in_specs=[pl.BlockSpec((TOP_K, td), lambda i: (0, i), memory_space=pltpu.SMEM),
                  pl.BlockSpec((td, hn.shape[1]), lambda i: (i, 0))],
        out_specs=pl.BlockSpec(memory_space=pl.ANY),
        out_shape=jax.ShapeDtypeStruct((n_rows, hn.shape[1]), hn.dtype),
        scratch_shapes=[pltpu.SemaphoreType.DMA(())],
        compiler_params=_params(("arbitrary",)),
        name="moe_dispatch",
    )(pos, hn)


def _expert_up_body(ie_ref, ib_ref, ins_ref, inr_ref, x_ref, w_ref, b_ref, o_ref, *, tn):
    w = pl.program_id(0)
    nsub = ins_ref[w]
    nrows = inr_ref[w]
    half = w_ref.shape[0] // 2

    for n in range(1, ITEM_SUBS + 1):
        @pl.when(nsub == n)
        def _(n=n):
            m = n * SUB
            row = lax.broadcasted_iota(I32, (m, 1), 0)
            xw = jnp.where(row < nrows, x_ref[0:m, :], jnp.uint32(0))
            lo, hi = _unpack_bf16_pairs(xw)
            lo, hi = lo.astype(BF16), hi.astype(BF16)
            hw = tn // 2

            def gate_up(c0):
                return (lax.dot_general(lo, w_ref[0:half, c0:c0 + hw].astype(BF16), _NN,
                                        preferred_element_type=F32)
                        + lax.dot_general(hi, w_ref[half:, c0:c0 + hw].astype(BF16), _NN,
                                          preferred_element_type=F32)
                        + b_ref[:, c0:c0 + hw])

            def swiglu(gate, up):
                gate = jnp.minimum(gate, SWIGLU_LIMIT)
                up = jnp.clip(up, -SWIGLU_LIMIT, SWIGLU_LIMIT)
                return (up + 1.0) * gate * jax.nn.sigmoid(SWIGLU_ALPHA * gate)

            first, second = gate_up(0), gate_up(hw)
            even = (lax.broadcasted_iota(I32, (m, hw), 1) & 1) == 0
            gate = jnp.where(even, first, pltpu.roll(second, 1, 1))
            up = jnp.where(even, pltpu.roll(first, hw - 1, 1), second)
            o_ref[0:m, :] = swiglu(gate, up).astype(o_ref.dtype)
            if m < ITEM_ROWS:
                o_ref[m:, :] = jnp.zeros((ITEM_ROWS - m, tn // 2), o_ref.dtype)


def _expert_up(items, n_items, xg, w_gu, b_gu, *, tn):
    item_e, item_blk, item_nsub, item_rows = items
    n_exp, d, f2 = w_gu.shape
    nj = f2 // tn
    n_rows = xg.shape[0]

    def jmap(j, ns, w):
        return jnp.where(ns[w] > 0, j, nj - 1)

    grid_spec = pltpu.PrefetchScalarGridSpec(
        num_scalar_prefetch=4,
        grid=(n_items, nj),
        in_specs=[pl.BlockSpec((ITEM_ROWS, xg.shape[1]), lambda w, j, ie, ib, ns, nr: (ib[w], 0)),
                  pl.BlockSpec((None, d, tn), lambda w, j, ie, ib, ns, nr: (ie[w], 0, jmap(j, ns, w))),
                  pl.BlockSpec((None, 1, tn), lambda w, j, ie, ib, ns, nr: (ie[w], 0, jmap(j, ns, w)))],
        out_specs=pl.BlockSpec((ITEM_ROWS, tn // 2), lambda w, j, ie, ib, ns, nr: (ib[w], jmap(j, ns, w))),
    )
    return pl.pallas_call(
        functools.partial(_expert_up_body, tn=tn),
        grid_spec=grid_spec,
        out_shape=jax.ShapeDtypeStruct((n_rows, f2 // 2), BF16),
        compiler_params=_params(("arbitrary", "arbitrary")),
        name="expert_gate_up",
    )(item_e, item_blk, item_nsub, item_rows, xg, w_gu, b_gu.reshape(n_exp, 1, f2))


def _interleave_rows_bf16(a, b):
    def rounded(x):
        bits = lax.bitcast_convert_type(x, U32)
        return bits + jnp.uint32(0x7FFF) + ((bits >> 16) & jnp.uint32(1))
    word = (rounded(b) & jnp.uint32(0xFFFF0000)) | (rounded(a) >> 16)
    return pltpu.bitcast(word, BF16)


def _expert_down_body(ie_ref, ib_ref, ins_ref, a_ref, w_ref, b_ref, o_ref, *, group):
    w = pl.program_id(0)
    nsub = ins_ref[w]

    for n in range(1, ITEM_SUBS + 1):
        @pl.when(nsub == n)
        def _(n=n):
            m = n * SUB
            wb = jnp.concatenate(
                [_interleave_rows_bf16(w_ref[g0:g0 + group // 2, :], w_ref[g0 + group // 2:g0 + group, :])
                 for g0 in range(0, w_ref.shape[0], group)], axis=0)
            y = lax.dot_general(a_ref[0:m, :], wb, _NN, preferred_element_type=F32)
            o_ref[0:m, :] = _pack_bf16_pairs(y + b_ref[...])
            if m < ITEM_ROWS:
                o_ref[m:, :] = jnp.zeros((ITEM_ROWS - m, o_ref.shape[1]), o_ref.dtype)


def _expert_down(items, n_items, act, w_d, b_d, *, group, tn):
    item_e, item_blk, item_nsub, _ = items
    n_exp, f, d = w_d.shape
    nj = d // tn
    n_rows = act.shape[0]

    def jmap(j, ns, w):
        return jnp.where(ns[w] > 0, j, nj - 1)

    grid_spec = pltpu.PrefetchScalarGridSpec(
        num_scalar_prefetch=3,
        grid=(n_items, nj),
        in_specs=[pl.BlockSpec((ITEM_ROWS, f), lambda w, j, ie, ib, ns: (ib[w], 0)),
                  pl.BlockSpec((None, f, tn), lambda w, j, ie, ib, ns: (ie[w], 0, jmap(j, ns, w))),
                  pl.BlockSpec((None, 1, tn), lambda w, j, ie, ib, ns: (ie[w], 0, jmap(j, ns, w)))],
        out_specs=pl.BlockSpec((ITEM_ROWS, tn // 2), lambda w, j, ie, ib, ns: (ib[w], jmap(j, ns, w))),
    )
    return pl.pallas_call(
        functools.partial(_expert_down_body, group=group),
        grid_spec=grid_spec,
        out_shape=jax.ShapeDtypeStruct((n_rows, d // 2), U32),
        compiler_params=_params(("arbitrary", "arbitrary")),
        name="expert_down",
    )(item_e, item_blk, item_nsub, act, w_d, b_d.reshape(n_exp, 1, d))


def _combine_body(pos_ref, posn_ref, wts_ref, x_ref, g_ref, yg_ref, o_ref, buf_ref, sem, *, tc, pw):
    i = pl.program_id(0)
    n = pl.num_programs(0)
    slot = lax.rem(i, 2)

    def row_copy(p_ref, s, t, k):
        return pltpu.make_async_copy(yg_ref.at[pl.ds(p_ref[k, t], 1), :],
                                     buf_ref.at[s, k, pl.ds(t, 1), :], sem.at[s])

    def start_tile(p_ref, s):
        def body(t, c):
            for k in range(TOP_K):
                row_copy(p_ref, s, t, k).start(priority=k % 2)
            return c
        lax.fori_loop(0, tc, body, 0, unroll=DMA_UNROLL)

    @pl.when(i == 0)
    def _():
        start_tile(pos_ref, 0)

    @pl.when(i + 1 < n)
    def _():
        start_tile(posn_ref, 1 - slot)

    def wait_body(t, c):
        for k in range(TOP_K):
            row_copy(pos_ref, slot, t, k).wait()
        return c
    lax.fori_loop(0, tc, wait_body, 0, unroll=DMA_UNROLL)

    parts = []
    for j in range(buf_ref.shape[3] // pw):
        lo = hi = None
        for k in range(TOP_K):
            wk = wts_ref[:, k:k + 1]
            l, h = _unpack_bf16_pairs(buf_ref[slot, k, :, j * pw:(j + 1) * pw])
            lo = wk * l if lo is None else lo + wk * l
            hi = wk * h if hi is None else hi + wk * h
        parts += [lo, hi]
    o_ref[...] = _rms(x_ref[...] + jnp.concatenate(parts, axis=1), g_ref[...])


def _combine(pos, wts_t, x1, g, yg, *, pw, tc=128):
    t, d = x1.shape
    nt = t // tc
    return pl.pallas_call(
        functools.partial(_combine_body, tc=tc, pw=pw),
        grid=(nt,),
        in_specs=[pl.BlockSpec((TOP_K, tc), lambda i: (0, i), memory_space=pltpu.SMEM),
                  pl.BlockSpec((TOP_K, tc), lambda i: (0, jnp.minimum(i + 1, nt - 1)), memory_space=pltpu.SMEM),
                  pl.BlockSpec((tc, TOP_K), lambda i: (i, 0)),
                  pl.BlockSpec((tc, d), lambda i: (i, 0)),
                  pl.BlockSpec((1, d), lambda i: (0, 0)),
                  pl.BlockSpec(memory_space=pl.ANY)],
        out_specs=pl.BlockSpec((tc, d), lambda i: (i, 0)),
        out_shape=jax.ShapeDtypeStruct((t, d), F32),
        scratch_shapes=[pltpu.VMEM((2, TOP_K, tc, d // 2), U32), pltpu.SemaphoreType.DMA((2,))],
        compiler_params=_params(("arbitrary",)),
        name="moe_combine_norm",
    )(pos, pos, wts_t, x1, g.reshape(1, d), yg)


def _plan_items(counts, n_assign):
    n_exp = counts.shape[0]
    max_items = n_assign // ITEM_ROWS + n_exp
    nsub_e = (counts + SUB - 1) // SUB
    nitem_e = (counts + ITEM_ROWS - 1) // ITEM_ROWS
    last_item_e = jnp.cumsum(nitem_e)
    first_item_e = last_item_e - nitem_e
    total = last_item_e[-1]
    w = jnp.arange(max_items, dtype=I32)
    valid = w < total
    e_w = jnp.minimum(jnp.searchsorted(last_item_e, w, side="right"), n_exp - 1).astype(I32)
    e_last = e_w[jnp.maximum(total - 1, 0)]
    e_w = jnp.where(valid, e_w, e_last)
    c_w = w - first_item_e[e_w]
    nsub_w = jnp.where(valid, jnp.clip(nsub_e[e_w] - ITEM_SUBS * c_w, 0, ITEM_SUBS), 0).astype(I32)
    rows_w = jnp.where(valid, jnp.clip(counts[e_w] - ITEM_ROWS * c_w, 0, ITEM_ROWS), 0).astype(I32)
    blk_w = jnp.where(valid, w, max_items).astype(I32)
    row_off_e = (first_item_e * ITEM_ROWS).astype(I32)
    return (e_w, blk_w, nsub_w, rows_w), total.astype(I32), row_off_e, (max_items + 1) * ITEM_ROWS


def kernel(x, positions, norm_mix, w_in, q_a_norm, w_q_b, kv_a_norm, w_kv_b, na_rpb, w_proj_a, w_proj_b, w_out, norm_ffn, w_router, b_router, w_gate_up, b_gate_up, w_down, b_down, norm_final):
    b, s, d = x.shape
    t = b * s
    na_w = NA_HEADS * NA_HEAD_DIM
    q_rank, kv_rank = q_a_norm.shape[1], kv_a_norm.shape[1]
    lat0 = 3 * na_w
    gate0 = lat0 + q_rank + kv_rank + MLA_ROPE_DIM
    xf = x.reshape(t, d)
    w_in_t = jnp.swapaxes(w_in, 1, 2)[0]

    hn = _rmsnorm(xf, norm_mix[0])
    qkv = _matmul_t(hn, w_in_t, row0=0, ncols=lat0, tm=2048, tn=512, out_dtype=BF16, name="proj_qkv")
    lat = _matmul_t(hn, w_in_t, row0=lat0, ncols=2048, tm=2048, tn=512, out_dtype=F32, name="proj_latent")
    half = MLA_ROPE_DIM // 2
    inv_freq = ROPE_THETA ** (-(jnp.arange(half, dtype=F32) * 2.0) / MLA_ROPE_DIM)
    inv_freq128 = jnp.tile(inv_freq, 128 // half).reshape(1, 128)
    cqn, ckvn, kpe, cos, sin = _mla_prep(lat, positions.reshape(t), q_a_norm[0], kv_a_norm[0], inv_freq128)
    qk_dim = MLA_NOPE_DIM + MLA_ROPE_DIM
    wq = w_q_b[0].reshape(q_rank, MLA_HEADS, qk_dim)
    wq = jnp.concatenate([wq[:, :, :MLA_NOPE_DIM].reshape(q_rank, -1),
                          wq[:, :, MLA_NOPE_DIM:].reshape(q_rank, -1)], axis=1)
    q2 = _qproj(cqn, wq, cos, sin, rope_col0=MLA_HEADS * MLA_NOPE_DIM, tm=2048, tn=1024)
    kv = _matmul(ckvn, w_kv_b[0], col0=0, ncols=w_kv_b.shape[2], tm=2048, tn=1024, out_dtype=BF16,
                 name="kv_proj")
    ob, gates, oa = _fused_attention(q2, kv, kpe, hn, w_in_t, qkv, *_na_bias_tables(na_rpb[0]),
                                     gate_row0=gate0, n_gate=2 * d, batch=b, seq=s)

    y = _merge(oa, ob, w_proj_a[0], w_proj_b[0], gates, tm=2048, tn=512)
    x1 = _outproj(y, w_out[0], xf, tm=2048, tn=512)

    hn_packed, idx, wts, rank, counts = _router(x1, norm_ffn[0], w_router[0].T, b_router[0])
    items, n_items, row_off, n_rows = _plan_items(counts[:, 0], t * TOP_K)
    onehot = idx[None] == jnp.arange(row_off.shape[0], dtype=I32)[:, None, None]
    pos = jnp.sum(jnp.where(onehot, row_off[:, None, None], 0), axis=0) + rank
    xg = _dispatch(pos, hn_packed, n_rows)
    up_tn, down_tn = 512, 1024
    act = _expert_up(items, n_items, xg, w_gate_up[0], b_gate_up[0], tn=up_tn)
    yg = _expert_down(items, n_items, act, w_down[0], b_down[0], group=up_tn // 2, tn=down_tn)
    out = _combine(pos, wts.T, x1, norm_final, yg, pw=down_tn // 2)
    return out.reshape(b, s, d)
```

```python
import functools

import numpy as np
import jax
import jax.numpy as jnp
from jax import lax
from jax.experimental import pallas as pl
from jax.experimental.pallas import tpu as pltpu

F32 = jnp.float32
BF16 = jnp.bfloat16
U32 = jnp.uint32
I32 = jnp.int32

GRID_W = 64
NA_HEADS = 16
NA_HEAD_DIM = 128
NA_WIN_ROWS = 8
NA_WIN_COLS = 16
MLA_HEADS = 16
MLA_NOPE_DIM = 128
MLA_ROPE_DIM = 64
MLA_V_DIM = 128
ROPE_THETA = 10000.0
TOP_K = 4
SWIGLU_LIMIT = 7.0
SWIGLU_ALPHA = 1.702
NORM_EPS = 1e-6

NEG_BIG = -1e30
LOG2_E = 1.4426950408889634

V7X_VMEM_BYTES = 64 * 1024 * 1024
VMEM_LIMIT = V7X_VMEM_BYTES - 4 * 1024 * 1024

SUB = 64
ITEM_SUBS = 20
ITEM_ROWS = SUB * ITEM_SUBS
DMA_UNROLL = 8

_NN = (((1,), (0,)), ((), ()))
_NT = (((1,), (1,)), ((), ()))


def _params(sem=None):
    return pltpu.CompilerParams(vmem_limit_bytes=VMEM_LIMIT, dimension_semantics=sem)


def _rms(x, g):
    return x * lax.rsqrt(jnp.mean(x * x, axis=-1, keepdims=True) + NORM_EPS) * g


def _pack_bf16_pairs(x):
    n = x.shape[1] // 2
    lo = lax.bitcast_convert_type(x[:, :n].astype(BF16).astype(F32), U32)
    hi = lax.bitcast_convert_type(x[:, n:].astype(BF16).astype(F32), U32)
    return (hi & jnp.uint32(0xFFFF0000)) | (lo >> 16)


def _unpack_bf16_pairs(w):
    return (lax.bitcast_convert_type(w << 16, F32),
            lax.bitcast_convert_type(w & jnp.uint32(0xFFFF0000), F32))


def _rmsnorm_body(x_ref, g_ref, o_ref):
    o_ref[...] = _rms(x_ref[...], g_ref[...]).astype(o_ref.dtype)


def _rmsnorm(x, g, *, tm=512):
    t, d = x.shape
    return pl.pallas_call(
        _rmsnorm_body,
        grid=(t // tm,),
        in_specs=[pl.BlockSpec((tm, d), lambda i: (i, 0)),
                  pl.BlockSpec((1, d), lambda i: (0, 0))],
        out_specs=pl.BlockSpec((tm, d), lambda i: (i, 0)),
        out_shape=jax.ShapeDtypeStruct((t, d), BF16),
        compiler_params=_params(("arbitrary",)),
        name="rmsnorm",
    )(x, g.reshape(1, d))


def _mm_body(a_ref, w_ref, o_ref):
    acc = lax.dot_general(a_ref[...], w_ref[...].astype(BF16), _NN, preferred_element_type=F32)
    o_ref[...] = acc.astype(o_ref.dtype)


def _matmul(a, w, *, col0, ncols, tm, tn, out_dtype, name):
    t, k = a.shape
    return pl.pallas_call(
        _mm_body,
        grid=(t // tm, ncols // tn),
        in_specs=[pl.BlockSpec((tm, k), lambda i, j: (i, 0), pipeline_mode=pl.Buffered(1)),
                  pl.BlockSpec((k, tn), lambda i, j: (0, j + col0 // tn))],
        out_specs=pl.BlockSpec((tm, tn), lambda i, j: (i, j)),
        out_shape=jax.ShapeDtypeStruct((t, ncols), out_dtype),
        compiler_params=_params(("arbitrary", "arbitrary")),
        name=name,
    )(a, w)


def _mm_t_body(a_ref, wt_ref, o_ref):
    acc = lax.dot_general(a_ref[...], wt_ref[...].astype(BF16), _NT, preferred_element_type=F32)
    o_ref[...] = acc.astype(o_ref.dtype)


def _matmul_t(a, w_t, *, row0, ncols, tm, tn, out_dtype, name):
    t, k = a.shape
    return pl.pallas_call(
        _mm_t_body,
        grid=(t // tm, ncols // tn),
        in_specs=[pl.BlockSpec((tm, k), lambda i, j: (i, 0), pipeline_mode=pl.Buffered(1)),
                  pl.BlockSpec((pl.Element(tn), pl.Element(k)),
                               lambda i, j: (pl.multiple_of(row0 + j * tn, 8), 0))],
        out_specs=pl.BlockSpec((tm, tn), lambda i, j: (i, j)),
        out_shape=jax.ShapeDtypeStruct((t, ncols), out_dtype),
        compiler_params=_params(("arbitrary", "arbitrary")),
        name=name,
    )(a, w_t)


def _qproj_body(a_ref, w_ref, cos_ref, sin_ref, o_ref, *, first_rope_block, tn):
    acc = lax.dot_general(a_ref[...], w_ref[...].astype(BF16), _NN, preferred_element_type=F32)
    j = pl.program_id(1)

    @pl.when(j < first_rope_block)
    def _():
        o_ref[...] = acc.astype(o_ref.dtype)

    @pl.when(j >= first_rope_block)
    def _():
        reps = tn // 128
        c = jnp.concatenate([cos_ref[...]] * reps, axis=1)
        s = jnp.concatenate([sin_ref[...]] * reps, axis=1)
        lane = lax.broadcasted_iota(I32, acc.shape, 1)
        first = (lane & (MLA_ROPE_DIM - 1)) < MLA_ROPE_DIM // 2
        half = MLA_ROPE_DIM // 2
        partner = jnp.where(first, pltpu.roll(acc, tn - half, 1), pltpu.roll(acc, half, 1))
        o_ref[...] = (acc * c + partner * s).astype(o_ref.dtype)


def _qproj(a, w, cos, sin, *, rope_col0, tm, tn):
    t, k = a.shape
    n = w.shape[1]
    return pl.pallas_call(
        functools.partial(_qproj_body, first_rope_block=rope_col0 // tn, tn=tn),
        grid=(t // tm, n // tn),
        in_specs=[pl.BlockSpec((tm, k), lambda i, j: (i, 0), pipeline_mode=pl.Buffered(1)),
                  pl.BlockSpec((k, tn), lambda i, j: (0, j)),
                  pl.BlockSpec((tm, 128), lambda i, j: (i, 0)),
                  pl.BlockSpec((tm, 128), lambda i, j: (i, 0))],
        out_specs=pl.BlockSpec((tm, tn), lambda i, j: (i, j)),
        out_shape=jax.ShapeDtypeStruct((t, n), BF16),
        compiler_params=_params(("arbitrary", "arbitrary")),
        name="q_proj_rope",
    )(a, w, cos, sin)


def _merge_body(oa_ref, ob_ref, wa_ref, wb_ref, ga_ref, gb_ref, o_ref):
    pa = lax.dot_general(oa_ref[...], wa_ref[...].astype(BF16), _NN, preferred_element_type=F32)
    pb = lax.dot_general(ob_ref[...], wb_ref[...].astype(BF16), _NN, preferred_element_type=F32)
    o_ref[...] = (ga_ref[...] * pa + gb_ref[...] * pb).astype(o_ref.dtype)


def _merge(oa, ob, wa, wb, gates, *, tm, tn):
    t, k = oa.shape
    n = wa.shape[1]
    nb = n // tn
    return pl.pallas_call(
        _merge_body,
        grid=(t // tm, nb),
        in_specs=[pl.BlockSpec((tm, k), lambda i, j: (i, 0), pipeline_mode=pl.Buffered(1)),
                  pl.BlockSpec((tm, k), lambda i, j: (i, 0), pipeline_mode=pl.Buffered(1)),
                  pl.BlockSpec((k, tn), lambda i, j: (0, j)),
                  pl.BlockSpec((k, tn), lambda i, j: (0, j)),
                  pl.BlockSpec((tm, tn), lambda i, j: (i, j)),
                  pl.BlockSpec((tm, tn), lambda i, j: (i, j + nb))],
        out_specs=pl.BlockSpec((tm, tn), lambda i, j: (i, j)),
        out_shape=jax.ShapeDtypeStruct((t, n), BF16),
        compiler_params=_params(("arbitrary", "arbitrary")),
        name="gated_merge",
    )(oa, ob, wa, wb, gates, gates)


def _outproj_body(a_ref, w_ref, r_ref, o_ref):
    acc = lax.dot_general(a_ref[...], w_ref[...].astype(BF16), _NN, preferred_element_type=F32)
    o_ref[...] = r_ref[...] + acc


def _outproj(a, w, resid, *, tm, tn):
    t, k = a.shape
    n = w.shape[1]
    return pl.pallas_call(
        _outproj_body,
        grid=(t // tm, n // tn),
        in_specs=[pl.BlockSpec((tm, k), lambda i, j: (i, 0), pipeline_mode=pl.Buffered(1)),
                  pl.BlockSpec((k, tn), lambda i, j: (0, j)),
                  pl.BlockSpec((tm, tn), lambda i, j: (i, j))],
        out_specs=pl.BlockSpec((tm, tn), lambda i, j: (i, j)),
        out_shape=jax.ShapeDtypeStruct((t, n), F32),
        compiler_params=_params(("arbitrary", "arbitrary")),
        name="out_proj_residual",
    )(a, w, resid)


def _mla_prep_body(lat_ref, pos_ref, qn_ref, kvn_ref, invf_ref,
                   cq_ref, ckv_ref, kpe_ref, cos_ref, sin_ref, *, q_rank, kv_rank):
    cq_ref[...] = _rms(lat_ref[:, 0:q_rank], qn_ref[...]).astype(cq_ref.dtype)
    ckv_ref[...] = _rms(lat_ref[:, q_rank:q_rank + kv_rank], kvn_ref[...]).astype(ckv_ref.dtype)
    kr = lat_ref[:, q_rank + kv_rank:q_rank + kv_rank + 128]
    ang = pos_ref[...].astype(F32) * invf_ref[...]
    c = jnp.cos(ang)
    s = jnp.sin(ang)
    lane = lax.broadcasted_iota(I32, ang.shape, 1)
    half = MLA_ROPE_DIM // 2
    first = (lane & (MLA_ROPE_DIM - 1)) < half
    s = jnp.where(first, -s, s)
    partner = jnp.where(first, pltpu.roll(kr, 128 - half, 1), pltpu.roll(kr, half, 1))
    kpe = jnp.where(lane < MLA_ROPE_DIM, kr * c + partner * s, 0.0)
    kpe_ref[:, 0:128] = kpe.astype(kpe_ref.dtype)
    kpe_ref[:, 128:256] = pltpu.roll(kpe, MLA_ROPE_DIM, 1).astype(kpe_ref.dtype)
    cos_ref[...] = c
    sin_ref[...] = s


def _mla_prep(lat, positions, q_norm, kv_norm, inv_freq128, *, tm=512):
    t = lat.shape[0]
    q_rank, kv_rank = q_norm.shape[0], kv_norm.shape[0]
    return pl.pallas_call(
        functools.partial(_mla_prep_body, q_rank=q_rank, kv_rank=kv_rank),
        grid=(t // tm,),
        in_specs=[pl.BlockSpec((tm, lat.shape[1]), lambda i: (i, 0)),
                  pl.BlockSpec((tm, 1), lambda i: (i, 0)),
                  pl.BlockSpec((1, q_rank), lambda i: (0, 0)),
                  pl.BlockSpec((1, kv_rank), lambda i: (0, 0)),
                  pl.BlockSpec((1, 128), lambda i: (0, 0))],
        out_specs=[pl.BlockSpec((tm, q_rank), lambda i: (i, 0)),
                   pl.BlockSpec((tm, kv_rank), lambda i: (i, 0)),
                   pl.BlockSpec((tm, 256), lambda i: (i, 0)),
                   pl.BlockSpec((tm, 128), lambda i: (i, 0)),
                   pl.BlockSpec((tm, 128), lambda i: (i, 0))],
        out_shape=[jax.ShapeDtypeStruct((t, q_rank), BF16),
                   jax.ShapeDtypeStruct((t, kv_rank), BF16),
                   jax.ShapeDtypeStruct((t, 256), BF16),
                   jax.ShapeDtypeStruct((t, 128), F32),
                   jax.ShapeDtypeStruct((t, 128), F32)],
        compiler_params=_params(("arbitrary",)),
        name="mla_prep",
    )(lat, positions.reshape(t, 1), q_norm.reshape(1, -1), kv_norm.reshape(1, -1), inv_freq128)


NA_Q_ROWS = 4
NA_K_ROWS = 12
NA_PAD_BLOCKS = NA_Q_ROWS
NA_BIAS_BLOCKS = 2 * NA_WIN_ROWS - 1 + 2 * NA_PAD_BLOCKS + 1


def _na_group(g, rows):
    r0 = NA_Q_ROWS * g
    w0 = min(max(r0 - NA_WIN_ROWS // 2, 0), rows - NA_K_ROWS)
    return r0, w0


def _na_bias_tables(rpb):
    n_heads = rpb.shape[0]
    kj = np.arange(GRID_W)[:, None]
    c = np.arange(GRID_W)[None, :]
    cs = np.clip(c - NA_WIN_COLS // 2, 0, GRID_W - NA_WIN_COLS)
    col_ok = (kj >= cs) & (kj < cs + NA_WIN_COLS)
    dc = kj - c + NA_WIN_COLS - 1
    pick = np.stack([(dc == dd) & col_ok for dd in range(2 * NA_WIN_COLS - 1)]).astype(np.float32)
    toep = jnp.einsum("hrd,dkc->hrkc", rpb.astype(F32), pick, precision=lax.Precision.HIGHEST)
    toep = jnp.where(col_ok[None, None], toep, NEG_BIG)
    n_off = 2 * NA_WIN_ROWS - 1
    blocks = jnp.pad(toep[:, ::-1], ((0, 0), (NA_PAD_BLOCKS, NA_BIAS_BLOCKS - NA_PAD_BLOCKS - n_off), (0, 0), (0, 0)),
                     constant_values=NEG_BIG)
    even = blocks.transpose(0, 2, 1, 3).reshape(n_heads, GRID_W, NA_BIAS_BLOCKS * GRID_W)
    odd = jnp.pad(even[:, :, GRID_W:], ((0, 0), (0, 0), (0, GRID_W)), constant_values=NEG_BIG)
    return even, odd


def _na_group_bias(even_ref, odd_ref, g, rows):
    r0, w0 = _na_group(g, rows)
    nq = NA_Q_ROWS * GRID_W
    slabs = []
    for k in range(NA_K_ROWS):
        ki = w0 + k
        bad = []
        for q in range(NA_Q_ROWS):
            rs = min(max(r0 + q - NA_WIN_ROWS // 2, 0), rows - NA_WIN_ROWS)
            if not rs <= ki < rs + NA_WIN_ROWS:
                bad.append(q)
        if len(bad) == NA_Q_ROWS:
            slabs.append(jnp.full((GRID_W, nq), NEG_BIG, F32))
            continue
        i0 = NA_PAD_BLOCKS + (NA_WIN_ROWS - 1) - (ki - r0)
        ref, first = (even_ref, i0) if i0 % 2 == 0 else (odd_ref, i0 - 1)
        slab = ref[0, :, first * GRID_W:first * GRID_W + nq]
        if bad:
            q_of_lane = lax.broadcasted_iota(I32, slab.shape, 1) // GRID_W
            outside = q_of_lane == bad[0]
            for q in bad[1:]:
                outside = outside | (q_of_lane == q)
            slab = jnp.where(outside, NEG_BIG, slab)
        slabs.append(slab)
    return jnp.concatenate(slabs, axis=0)


def _attn_body(qn_ref, qp_ref, kv_ref, kpe_ref, a_ref, wt_ref, nq_ref, nk_ref, nv_ref, even_ref, odd_ref,
               ob_ref, g_ref, oa_ref, kf_ref, vt_ref, *, mla_scale, na_scale, rows, steps_per_na_head):
    qi = pl.program_id(2)

    @pl.when(qi == 0)
    def _():
        for hh in range(2):
            kf_ref[hh, :, 0:128] = kv_ref[:, 256 * hh:256 * hh + 128]
            kf_ref[hh, :, 128:256] = kpe_ref[:, 128 * hh:128 * hh + 128]
            v = kv_ref[:, 256 * hh + 128:256 * hh + 256]
            vt_ref[hh] = v.astype(F32).T.astype(BF16)

    n_groups = rows // NA_Q_ROWS
    per_step = n_groups // steps_per_na_head
    nqq, nkk = NA_Q_ROWS * GRID_W, NA_K_ROWS * GRID_W

    def na_groups(nvt, groups):
        for g in groups:
            r0, w0 = _na_group(g, rows)
            kwin = nk_ref[w0 * GRID_W:w0 * GRID_W + nkk, :]
            qg = nq_ref[r0 * GRID_W:r0 * GRID_W + nqq, :]
            st = (lax.dot_general(kwin, qg, _NT, preferred_element_type=F32) * na_scale
                  + _na_group_bias(even_ref, odd_ref, g, rows))
            m = jnp.max(st, axis=0, keepdims=True)
            e = jnp.exp(st - m)
            l = jnp.sum(e, axis=0, keepdims=True)
            ot = lax.dot_general(nvt[:, w0 * GRID_W:w0 * GRID_W + nkk], e.astype(BF16), _NN,
                                 preferred_element_type=F32)
            oa_ref[r0 * GRID_W:r0 * GRID_W + nqq, :] = (ot / l).T.astype(oa_ref.dtype)

    for part in range(steps_per_na_head):
        @pl.when(lax.rem(qi, steps_per_na_head) == part)
        def _(part=part):
            groups = list(range(part * per_step, (part + 1) * per_step))
            nvt = nv_ref[...].astype(F32).T.astype(BF16)
            lane = lax.broadcasted_iota(I32, qp_ref.shape, 1)
            scores = []
            for hh in range(2):
                qp = qp_ref[...]
                keep = (lane >= MLA_ROPE_DIM) if hh else (lane < MLA_ROPE_DIM)
                qp = jnp.where(keep, qp, jnp.zeros_like(qp))
                qf = jnp.concatenate([qn_ref[:, 128 * hh:128 * hh + 128], qp], axis=1)
                scores.append(lax.dot_general(kf_ref[hh], qf, _NT, preferred_element_type=F32))
            gh = g_ref.shape[1] // 2
            for hh in range(2):
                g_ref[:, gh * hh:gh * (hh + 1)] = jax.nn.sigmoid(
                    lax.dot_general(a_ref[...], wt_ref[gh * hh:gh * (hh + 1), :].astype(BF16), _NT,
                                    preferred_element_type=F32)).astype(g_ref.dtype)
                st = scores[hh]
                m = jnp.max(st, axis=0, keepdims=True)
                e = jnp.exp2((st - m) * (mla_scale * LOG2_E))
                l = jnp.sum(e, axis=0, keepdims=True)
                ot = lax.dot_general(vt_ref[hh], e.astype(BF16), _NN, preferred_element_type=F32)
                ob_ref[:, 128 * hh:128 * hh + 128] = (ot / l).T.astype(ob_ref.dtype)
                na_groups(nvt, groups[hh * (per_step // 2):(hh + 1) * (per_step // 2)])


def _fused_attention(q2, kv, kpe, hn, w_t, qkv, bias_even, bias_odd, *, gate_row0, n_gate, batch, seq,
                     tq=512, gtn=512):
    t = q2.shape[0]
    k = hn.shape[1]
    nq = seq // tq
    hp = MLA_HEADS // 2
    nope_w = MLA_HEADS * MLA_NOPE_DIM
    rows = seq // GRID_W
    n_steps = batch * hp * nq
    gcols = n_gate // gtn
    gtm = t * gcols // n_steps
    steps_per_na_head = nq * hp // NA_HEADS
    assert steps_per_na_head * NA_HEADS == nq * hp and (rows // NA_Q_ROWS) % (2 * steps_per_na_head) == 0

    def step(b, h, q):
        return (b * hp + h) * nq + q

    def na_head(h, q):
        return (h * nq + q) // steps_per_na_head

    bias_spec = pl.BlockSpec((1, GRID_W, NA_BIAS_BLOCKS * GRID_W), lambda b, h, q: (na_head(h, q), 0, 0))
    return pl.pallas_call(
        functools.partial(_attn_body, mla_scale=float((MLA_NOPE_DIM + MLA_ROPE_DIM) ** -0.5),
                          na_scale=float(NA_HEAD_DIM ** -0.5), rows=rows, steps_per_na_head=steps_per_na_head),
        grid=(batch, hp, nq),
        in_specs=[pl.BlockSpec((tq, 256), lambda b, h, q: (b * nq + q, h)),
                  pl.BlockSpec((tq, 128), lambda b, h, q: (b * nq + q, nope_w // 128 + h)),
                  pl.BlockSpec((seq, 512), lambda b, h, q: (b, h)),
                  pl.BlockSpec((seq, 256), lambda b, h, q: (b, 0)),
                  pl.BlockSpec((gtm, k), lambda b, h, q: (step(b, h, q) // gcols, 0),
                               pipeline_mode=pl.Buffered(1)),
                  pl.BlockSpec((pl.Element(gtn), pl.Element(k)),
                               lambda b, h, q: (pl.multiple_of(gate_row0 + (step(b, h, q) % gcols) * gtn, 8), 0)),
                  pl.BlockSpec((seq, NA_HEAD_DIM), lambda b, h, q: (b, na_head(h, q))),
                  pl.BlockSpec((seq, NA_HEAD_DIM), lambda b, h, q: (b, NA_HEADS + na_head(h, q))),
                  pl.BlockSpec((seq, NA_HEAD_DIM), lambda b, h, q: (b, 2 * NA_HEADS + na_head(h, q))),
                  bias_spec, bias_spec],
        out_specs=[pl.BlockSpec((tq, 256), lambda b, h, q: (b * nq + q, h)),
                   pl.BlockSpec((gtm, gtn), lambda b, h, q: (step(b, h, q) // gcols, step(b, h, q) % gcols)),
                   pl.BlockSpec((seq, NA_HEAD_DIM), lambda b, h, q: (b, na_head(h, q)))],
        out_shape=[jax.ShapeDtypeStruct((t, MLA_HEADS * MLA_V_DIM), BF16),
                   jax.ShapeDtypeStruct((t, n_gate), BF16),
                   jax.ShapeDtypeStruct((t, NA_HEADS * NA_HEAD_DIM), BF16)],
        scratch_shapes=[pltpu.VMEM((2, seq, 256), BF16), pltpu.VMEM((2, 128, seq), BF16)],
        compiler_params=_params(("arbitrary", "arbitrary", "arbitrary")),
        name="fused_attention_gates",
    )(q2, q2, kv, kpe, hn, w_t, qkv, qkv, qkv, bias_even, bias_odd)


def _router_body(x_ref, g_ref, wr_ref, br_ref, hn_ref, idx_ref, wts_ref, rank_ref, cnt_ref, carry_ref,
                 *, n_exp, tr):
    i = pl.program_id(0)

    @pl.when(i == 0)
    def _():
        carry_ref[...] = jnp.zeros_like(carry_ref)

    y = _rms(x_ref[...], g_ref[...])
    yb = y.astype(BF16)
    hn_ref[...] = _pack_bf16_pairs(y)

    logits = lax.dot_general(wr_ref[...].astype(BF16), yb, _NT, preferred_element_type=F32) + br_ref[...]
    eid = lax.broadcasted_iota(I32, (n_exp, tr), 0).astype(F32)
    work = logits
    vals, sels = [], []
    for k in range(TOP_K):
        m = jnp.max(work, axis=0, keepdims=True)
        first = jnp.min(jnp.where(work == m, eid, float(n_exp)), axis=0, keepdims=True)
        sel = eid == first
        vals.append(m)
        sels.append(sel)
        idx_ref[k:k + 1, :] = first.astype(I32)
        work = jnp.where(sel, -jnp.inf, work)
    es = [jnp.exp(v - vals[0]) for v in vals]
    denom = es[0] + es[1] + es[2] + es[3]
    for k in range(TOP_K):
        wts_ref[k:k + 1, :] = es[k] / denom

    chosen = jnp.zeros((n_exp, tr), F32)
    for sel in sels:
        chosen = chosen + sel.astype(F32)
    before = (lax.broadcasted_iota(I32, (tr, tr), 0) < lax.broadcasted_iota(I32, (tr, tr), 1)).astype(BF16)
    carry = carry_ref[:, 0:1]
    base = lax.dot_general(chosen.astype(BF16), before, _NN, preferred_element_type=F32) + carry
    for k in range(TOP_K):
        rank_ref[k:k + 1, :] = jnp.sum(jnp.where(sels[k], base, 0.0), axis=0, keepdims=True).astype(I32)
    total = carry + jnp.sum(chosen, axis=1, keepdims=True)
    carry_ref[...] = jnp.broadcast_to(total, carry_ref.shape)
    cnt_ref[...] = jnp.broadcast_to(total, cnt_ref.shape).astype(I32)


def _router(x1, g, wr_t, br, *, tr=512):
    t, d = x1.shape
    n_exp = wr_t.shape[0]
    return pl.pallas_call(
        functools.partial(_router_body, n_exp=n_exp, tr=tr),
        grid=(t // tr,),
        in_specs=[pl.BlockSpec((tr, d), lambda i: (i, 0)),
                  pl.BlockSpec((1, d), lambda i: (0, 0)),
                  pl.BlockSpec((n_exp, d), lambda i: (0, 0)),
                  pl.BlockSpec((n_exp, 1), lambda i: (0, 0))],
        out_specs=[pl.BlockSpec((tr, d // 2), lambda i: (i, 0)),
                   pl.BlockSpec((TOP_K, tr), lambda i: (0, i)),
                   pl.BlockSpec((TOP_K, tr), lambda i: (0, i)),
                   pl.BlockSpec((TOP_K, tr), lambda i: (0, i)),
                   pl.BlockSpec((n_exp, 128), lambda i: (0, 0))],
        out_shape=[jax.ShapeDtypeStruct((t, d // 2), U32),
                   jax.ShapeDtypeStruct((TOP_K, t), I32),
                   jax.ShapeDtypeStruct((TOP_K, t), F32),
                   jax.ShapeDtypeStruct((TOP_K, t), I32),
                   jax.ShapeDtypeStruct((n_exp, 128), I32)],
        scratch_shapes=[pltpu.VMEM((n_exp, 128), F32)],
        compiler_params=_params(("arbitrary",)),
        name="router_topk",
    )(x1, g.reshape(1, d), wr_t, br.reshape(n_exp, 1))


def _dispatch_body(pos_ref, hn_ref, xg_ref, sem, *, td):
    def row_copy(t, k):
        return pltpu.make_async_copy(hn_ref.at[pl.ds(t, 1), :],
                                     xg_ref.at[pl.ds(pos_ref[k, t], 1), :], sem)

    def start(t, c):
        for k in range(TOP_K):
            row_copy(t, k).start(priority=k % 2)
        return c

    def wait(t, c):
        for k in range(TOP_K):
            row_copy(t, k).wait()
        return c

    lax.fori_loop(0, td, start, 0, unroll=DMA_UNROLL)
    lax.fori_loop(0, td, wait, 0, unroll=DMA_UNROLL)


def _dispatch(pos, hn, n_rows, *, td=512):
    t = hn.shape[0]
    return pl.pallas_call(
        functools.partial(_dispatch_body, td=td),
        grid=(t // td,),
        in_specs=[pl.BlockSpec((TOP_K, td), lambda i: (0, i), memory_space=pltpu.SMEM),
                  pl.BlockSpec((td, hn.shape[1]), lambda i: (i, 0))],
        out_specs=pl.BlockSpec(memory_space=pl.ANY),
        out_shape=jax.ShapeDtypeStruct((n_rows, hn.shape[1]), hn.dtype),
        scratch_shapes=[pltpu.SemaphoreType.DMA(())],
        compiler_params=_params(("arbitrary",)),
        name="moe_dispatch",
    )(pos, hn)


def _expert_up_body(ie_ref, ib_ref, ins_ref, inr_ref, x_ref, w_ref, b_ref, o_ref, *, tn):
    w = pl.program_id(0)
    nsub = ins_ref[w]
    nrows = inr_ref[w]
    half = w_ref.shape[0] // 2

    for n in range(1, ITEM_SUBS + 1):
        @pl.when(nsub == n)
        def _(n=n):
            m = n * SUB
            row = lax.broadcasted_iota(I32, (m, 1), 0)
            xw = jnp.where(row < nrows, x_ref[0:m, :], jnp.uint32(0))
            lo, hi = _unpack_bf16_pairs(xw)
            lo, hi = lo.astype(BF16), hi.astype(BF16)
            hw = tn // 2

            def gate_up(c0):
                return (lax.dot_general(lo, w_ref[0:half, c0:c0 + hw].astype(BF16), _NN,
                                        preferred_element_type=F32)
                        + lax.dot_general(hi, w_ref[half:, c0:c0 + hw].astype(BF16), _NN,
                                          preferred_element_type=F32)
                        + b_ref[:, c0:c0 + hw])

            def swiglu(gate, up):
                gate = jnp.minimum(gate, SWIGLU_LIMIT)
                up = jnp.clip(up, -SWIGLU_LIMIT, SWIGLU_LIMIT)
                return (up + 1.0) * gate * jax.nn.sigmoid(SWIGLU_ALPHA * gate)

            first, second = gate_up(0), gate_up(hw)
            even = (lax.broadcasted_iota(I32, (m, hw), 1) & 1) == 0
            gate = jnp.where(even, first, pltpu.roll(second, 1, 1))
            up = jnp.where(even, pltpu.roll(first, hw - 1, 1), second)
            o_ref[0:m, :] = swiglu(gate, up).astype(o_ref.dtype)
            if m < ITEM_ROWS:
                o_ref[m:, :] = jnp.zeros((ITEM_ROWS - m, tn // 2), o_ref.dtype)


def _expert_up(items, n_items, xg, w_gu, b_gu, *, tn):
    item_e, item_blk, item_nsub, item_rows = items
    n_exp, d, f2 = w_gu.shape
    nj = f2 // tn
    n_rows = xg.shape[0]

    def jmap(j, ns, w):
        return jnp.where(ns[w] > 0, j, nj - 1)

    grid_spec = pltpu.PrefetchScalarGridSpec(
        num_scalar_prefetch=4,
        grid=(n_items, nj),
        in_specs=[pl.BlockSpec((ITEM_ROWS, xg.shape[1]), lambda w, j, ie, ib, ns, nr: (ib[w], 0)),
                  pl.BlockSpec((None, d, tn), lambda w, j, ie, ib, ns, nr: (ie[w], 0, jmap(j, ns, w))),
                  pl.BlockSpec((None, 1, tn), lambda w, j, ie, ib, ns, nr: (ie[w], 0, jmap(j, ns, w)))],
        out_specs=pl.BlockSpec((ITEM_ROWS, tn // 2), lambda w, j, ie, ib, ns, nr: (ib[w], jmap(j, ns, w))),
    )
    return pl.pallas_call(
        functools.partial(_expert_up_body, tn=tn),
        grid_spec=grid_spec,
        out_shape=jax.ShapeDtypeStruct((n_rows, f2 // 2), BF16),
        compiler_params=_params(("arbitrary", "arbitrary")),
        name="expert_gate_up",
    )(item_e, item_blk, item_nsub, item_rows, xg, w_gu, b_gu.reshape(n_exp, 1, f2))


def _interleave_rows_bf16(a, b):
    def rounded(x):
        bits = lax.bitcast_convert_type(x, U32)
        return bits + jnp.uint32(0x7FFF) + ((bits >> 16) & jnp.uint32(1))
    word = (rounded(b) & jnp.uint32(0xFFFF0000)) | (rounded(a) >> 16)
    return pltpu.bitcast(word, BF16)


def _expert_down_body(ie_ref, ib_ref, ins_ref, a_ref, w_ref, b_ref, o_ref, *, group):
    w = pl.program_id(0)
    nsub = ins_ref[w]

    for n in range(1, ITEM_SUBS + 1):
        @pl.when(nsub == n)
        def _(n=n):
            m = n * SUB
            wb = jnp.concatenate(
                [_interleave_rows_bf16(w_ref[g0:g0 + group // 2, :], w_ref[g0 + group // 2:g0 + group, :])
                 for g0 in range(0, w_ref.shape[0], group)], axis=0)
            y = lax.dot_general(a_ref[0:m, :], wb, _NN, preferred_element_type=F32)
            o_ref[0:m, :] = _pack_bf16_pairs(y + b_ref[...])
            if m < ITEM_ROWS:
                o_ref[m:, :] = jnp.zeros((ITEM_ROWS - m, o_ref.shape[1]), o_ref.dtype)


def _expert_down(items, n_items, act, w_d, b_d, *, group, tn):
    item_e, item_blk, item_nsub, _ = items
    n_exp, f, d = w_d.shape
    nj = d // tn
    n_rows = act.shape[0]

    def jmap(j, ns, w):
        return jnp.where(ns[w] > 0, j, nj - 1)

    grid_spec = pltpu.PrefetchScalarGridSpec(
        num_scalar_prefetch=3,
        grid=(n_items, nj),
        in_specs=[pl.BlockSpec((ITEM_ROWS, f), lambda w, j, ie, ib, ns: (ib[w], 0)),
                  pl.BlockSpec((None, f, tn), lambda w, j, ie, ib, ns: (ie[w], 0, jmap(j, ns, w))),
                  pl.BlockSpec((None, 1, tn), lambda w, j, ie, ib, ns: (ie[w], 0, jmap(j, ns, w)))],
        out_specs=pl.BlockSpec((ITEM_ROWS, tn // 2), lambda w, j, ie, ib, ns: (ib[w], jmap(j, ns, w))),
    )
    return pl.pallas_call(
        functools.partial(_expert_down_body, group=group),
        grid_spec=grid_spec,
        out_shape=jax.ShapeDtypeStruct((n_rows, d // 2), U32),
        compiler_params=_params(("arbitrary", "arbitrary")),
        name="expert_down",
    )(item_e, item_blk, item_nsub, act, w_d, b_d.reshape(n_exp, 1, d))


def _combine_body(pos_ref, posn_ref, wts_ref, x_ref, g_ref, yg_ref, o_ref, buf_ref, sem, *, tc, pw):
    i = pl.program_id(0)
    n = pl.num_programs(0)
    slot = lax.rem(i, 2)

    def row_copy(p_ref, s, t, k):
        return pltpu.make_async_copy(yg_ref.at[pl.ds(p_ref[k, t], 1), :],
                                     buf_ref.at[s, k, pl.ds(t, 1), :], sem.at[s])

    def start_tile(p_ref, s):
        def body(t, c):
            for k in range(TOP_K):
                row_copy(p_ref, s, t, k).start(priority=k % 2)
            return c
        lax.fori_loop(0, tc, body, 0, unroll=DMA_UNROLL)

    @pl.when(i == 0)
    def _():
        start_tile(pos_ref, 0)

    @pl.when(i + 1 < n)
    def _():
        start_tile(posn_ref, 1 - slot)

    def wait_body(t, c):
        for k in range(TOP_K):
            row_copy(pos_ref, slot, t, k).wait()
        return c
    lax.fori_loop(0, tc, wait_body, 0, unroll=DMA_UNROLL)

    parts = []
    for j in range(buf_ref.shape[3] // pw):
        lo = hi = None
        for k in range(TOP_K):
            wk = wts_ref[:, k:k + 1]
            l, h = _unpack_bf16_pairs(buf_ref[slot, k, :, j * pw:(j + 1) * pw])
            lo = wk * l if lo is None else lo + wk * l
            hi = wk * h if hi is None else hi + wk * h
        parts += [lo, hi]
    o_ref[...] = _rms(x_ref[...] + jnp.concatenate(parts, axis=1), g_ref[...])


def _combine(pos, wts_t, x1, g, yg, *, pw, tc=256):
    t, d = x1.shape
    nt = t // tc
    return pl.pallas_call(
        functools.partial(_combine_body, tc=tc, pw=pw),
        grid=(nt,),
        in_specs=[pl.BlockSpec((TOP_K, tc), lambda i: (0, i), memory_space=pltpu.SMEM),
                  pl.BlockSpec((TOP_K, tc), lambda i: (0, jnp.minimum(i + 1, nt - 1)), memory_space=pltpu.SMEM),
                  pl.BlockSpec((tc, TOP_K), lambda i: (i, 0)),
                  pl.BlockSpec((tc, d), lambda i: (i, 0)),
                  pl.BlockSpec((1, d), lambda i: (0, 0)),
                  pl.BlockSpec(memory_space=pl.ANY)],
        out_specs=pl.BlockSpec((tc, d), lambda i: (i, 0)),
        out_shape=jax.ShapeDtypeStruct((t, d), F32),
        scratch_shapes=[pltpu.VMEM((2, TOP_K, tc, d // 2), U32), pltpu.SemaphoreType.DMA((2,))],
        compiler_params=_params(("arbitrary",)),
        name="moe_combine_norm",
    )(pos, pos, wts_t, x1, g.reshape(1, d), yg)


def _plan_items(counts, n_assign):
    n_exp = counts.shape[0]
    max_items = n_assign // ITEM_ROWS + n_exp
    nsub_e = (counts + SUB - 1) // SUB
    nitem_e = (counts + ITEM_ROWS - 1) // ITEM_ROWS
    last_item_e = jnp.cumsum(nitem_e)
    first_item_e = last_item_e - nitem_e
    total = last_item_e[-1]
    w = jnp.arange(max_items, dtype=I32)
    valid = w < total
    e_w = jnp.minimum(jnp.searchsorted(last_item_e, w, side="right"), n_exp - 1).astype(I32)
    e_last = e_w[jnp.maximum(total - 1, 0)]
    e_w = jnp.where(valid, e_w, e_last)
    c_w = w - first_item_e[e_w]
    nsub_w = jnp.where(valid, jnp.clip(nsub_e[e_w] - ITEM_SUBS * c_w, 0, ITEM_SUBS), 0).astype(I32)
    rows_w = jnp.where(valid, jnp.clip(counts[e_w] - ITEM_ROWS * c_w, 0, ITEM_ROWS), 0).astype(I32)
    blk_w = jnp.where(valid, w, max_items).astype(I32)
    row_off_e = (first_item_e * ITEM_ROWS).astype(I32)
    return (e_w, blk_w, nsub_w, rows_w), total.astype(I32), row_off_e, (max_items + 1) * ITEM_ROWS


def kernel(x, positions, norm_mix, w_in, q_a_norm, w_q_b, kv_a_norm, w_kv_b, na_rpb, w_proj_a, w_proj_b, w_out, norm_ffn, w_router, b_router, w_gate_up, b_gate_up, w_down, b_down, norm_final):
    b, s, d = x.shape
    t = b * s
    na_w = NA_HEADS * NA_HEAD_DIM
    q_rank, kv_rank = q_a_norm.shape[1], kv_a_norm.shape[1]
    lat0 = 3 * na_w
    gate0 = lat0 + q_rank + kv_rank + MLA_ROPE_DIM
    xf = x.reshape(t, d)
    w_in_t = jnp.swapaxes(w_in, 1, 2)[0]

    hn = _rmsnorm(xf, norm_mix[0])
    qkv = _matmul_t(hn, w_in_t, row0=0, ncols=lat0, tm=2048, tn=512, out_dtype=BF16, name="proj_qkv")
    lat = _matmul_t(hn, w_in_t, row0=lat0, ncols=2048, tm=2048, tn=512, out_dtype=F32, name="proj_latent")
    half = MLA_ROPE_DIM // 2
    inv_freq = ROPE_THETA ** (-(jnp.arange(half, dtype=F32) * 2.0) / MLA_ROPE_DIM)
    inv_freq128 = jnp.tile(inv_freq, 128 // half).reshape(1, 128)
    cqn, ckvn, kpe, cos, sin = _mla_prep(lat, positions.reshape(t), q_a_norm[0], kv_a_norm[0], inv_freq128)
    qk_dim = MLA_NOPE_DIM + MLA_ROPE_DIM
    wq = w_q_b[0].reshape(q_rank, MLA_HEADS, qk_dim)
    wq = jnp.concatenate([wq[:, :, :MLA_NOPE_DIM].reshape(q_rank, -1),
                          wq[:, :, MLA_NOPE_DIM:].reshape(q_rank, -1)], axis=1)
    q2 = _qproj(cqn, wq, cos, sin, rope_col0=MLA_HEADS * MLA_NOPE_DIM, tm=2048, tn=1024)
    kv = _matmul(ckvn, w_kv_b[0], col0=0, ncols=w_kv_b.shape[2], tm=2048, tn=1024, out_dtype=BF16,
                 name="kv_proj")
    ob, gates, oa = _fused_attention(q2, kv, kpe, hn, w_in_t, qkv, *_na_bias_tables(na_rpb[0]),
                                     gate_row0=gate0, n_gate=2 * d, batch=b, seq=s)

    y = _merge(oa, ob, w_proj_a[0], w_proj_b[0], gates, tm=2048, tn=512)
    x1 = _outproj(y, w_out[0], xf, tm=2048, tn=512)

    hn_packed, idx, wts, rank, counts = _router(x1, norm_ffn[0], w_router[0].T, b_router[0])
    items, n_items, row_off, n_rows = _plan_items(counts[:, 0], t * TOP_K)
    onehot = idx[None] == jnp.arange(row_off.shape[0], dtype=I32)[:, None, None]
    pos = jnp.sum(jnp.where(onehot, row_off[:, None, None], 0), axis=0) + rank
    xg = _dispatch(pos, hn_packed, n_rows)
    up_tn, down_tn = 512, 1024
    act = _expert_up(items, n_items, xg, w_gate_up[0], b_gate_up[0], tn=up_tn)
    yg = _expert_down(items, n_items, act, w_down[0], b_down[0], group=up_tn // 2, tn=down_tn)
    out = _combine(pos, wts.T, x1, norm_final, yg, pw=down_tn // 2)
    return out.reshape(b, s, d)
```
